```python
import jax, jax.numpy as jnp
from jax import lax
import numpy as np

D_MODEL = 1024
BATCH = 32
SEQ = 2048
DEPTH = 1
DEC_BATCH = 128
DEC_SEQ = 1
PAST_LEN = 8192
PAGE_SIZE = 128

HEAD_DIM = 64
N_HEADS_A = 8
N_HEADS_B = 8
WIDTH_A = N_HEADS_A * HEAD_DIM
WIDTH_B = N_HEADS_B * HEAD_DIM
MIX_WIDTH = WIDTH_A + WIDTH_B
IN_COLS = 2 * WIDTH_A + 3 * WIDTH_B
CHUNK = 128
MOBA_BLOCK = 256
MOBA_TOPK = 3
Q_CHUNK = 128
D_FF = 4 * D_MODEL
EPS = 1e-6
NEG = -1e30

kernel_name = "hymba_chunkmlp_moba_decode_step"


def rms_norm(x, g):
    xf = x.astype(jnp.float32)
    return xf * lax.rsqrt(jnp.mean(xf * xf, axis=-1, keepdims=True) + EPS) * g.astype(jnp.float32)


def pad_blocks(x):
    pad = (-x.shape[1]) % MOBA_BLOCK
    return jnp.pad(x, ((0, 0), (0, pad), (0, 0), (0, 0)))


def chunk_mlp(u, v, vn_g, w_s, b_s):
    B, L, _ = u.shape
    u = jax.nn.gelu(u)
    v = rms_norm(jax.nn.gelu(v).reshape(B, L, N_HEADS_A, HEAD_DIM), vn_g)
    n_chunks = -(-L // CHUNK)
    vp = jnp.pad(v, ((0, 0), (0, n_chunks * CHUNK - L), (0, 0), (0, 0)))
    vp = vp.reshape(B, n_chunks, CHUNK, N_HEADS_A, HEAD_DIM)
    causal = jnp.tril(jnp.ones((CHUNK, CHUNK), dtype=bool))
    ws = jnp.where(causal[None], w_s, 0.0)
    mixed = jnp.einsum('hts,bnshd->bnthd', ws, vp) + b_s.T[None, None, :, :, None]
    mixed = mixed.reshape(B, n_chunks * CHUNK, N_HEADS_A, HEAD_DIM)[:, :L]
    return u * mixed.reshape(B, L, WIDTH_A), v


def moba_seq(q, k_all, v_all, q_start, n_valid):
    Q = q.shape[0]
    nb = k_all.shape[0] // MOBA_BLOCK
    kb = k_all.reshape(nb, MOBA_BLOCK, N_HEADS_B, HEAD_DIM).transpose(2, 0, 1, 3)
    vb = v_all.reshape(nb, MOBA_BLOCK, N_HEADS_B, HEAD_DIM).transpose(2, 0, 1, 3)
    k_mean = jnp.mean(kb.astype(jnp.float32), axis=2)
    top = min(MOBA_TOPK, nb)
    qc = min(Q_CHUNK, Q)
    nq = -(-Q // qc)
    qp = jnp.pad(q, ((0, nq * qc - Q), (0, 0), (0, 0))).reshape(nq, qc, N_HEADS_B, HEAD_DIM)
    pos = jnp.minimum(q_start + jnp.arange(nq * qc, dtype=jnp.int32), n_valid - 1).reshape(nq, qc)
    blk_ids = jnp.arange(nb, dtype=jnp.int32)
    key_off = jnp.arange(MOBA_BLOCK, dtype=jnp.int32)
    h_ids = jnp.arange(N_HEADS_B, dtype=jnp.int32)[None, :, None]
    scale = HEAD_DIM ** -0.5

    def one_chunk(args):
        qq, pp = args
        own = pp // MOBA_BLOCK
        gate = jnp.einsum('qhd,hnd->qhn', qq, k_mean)
        past = blk_ids[None, None, :] < own[:, None, None]
        _, sel = lax.top_k(jnp.where(past, gate, NEG), top)
        sel = sel.astype(jnp.int32)
        own_b = jnp.broadcast_to(own[:, None, None], (qc, N_HEADS_B, 1))
        idx = jnp.concatenate([sel, own_b], axis=-1)
        ok = jnp.concatenate([sel < own[:, None, None], jnp.ones(own_b.shape, dtype=bool)], axis=-1)
        kg = kb[h_ids, idx]
        vg = vb[h_ids, idx]
        key_pos = idx[..., None] * MOBA_BLOCK + key_off
        mask = ok[..., None] & (key_pos <= pp[:, None, None, None])
        logits = jnp.einsum('qhd,qhjkd->qhjk', qq, kg.astype(jnp.float32)) * scale
        logits = jnp.where(mask, logits, NEG).reshape(qc, N_HEADS_B, -1)
        p = jax.nn.softmax(logits, axis=-1).reshape(mask.shape)
        return jnp.einsum('qhjk,qhjkd->qhd', p, vg.astype(jnp.float32))

    out = lax.map(one_chunk, (qp, pos))
    return out.reshape(nq * qc, N_HEADS_B, HEAD_DIM)[:Q]


def layer(x, c, p, attend):
    (norm_mix_g, norm_ffn_g, w_ada, b_ada, w_in, qn_g, kn_g, vn_g, w_s, b_s,
     on_a, on_b, w_out, w1, w2) = p
    B, L, _ = x.shape
    mod = (jax.nn.silu(c.astype(jnp.float32)) @ w_ada + b_ada)[:, None, :]
    sh_m, sc_m, g_m, sh_f, sc_f, g_f = jnp.split(mod, 6, axis=-1)
    h = rms_norm(x, norm_mix_g) * (1.0 + sc_m) + sh_m
    z = h @ w_in
    ua, va, q, k, v = jnp.split(z, [WIDTH_A, 2 * WIDTH_A, 2 * WIDTH_A + WIDTH_B, 2 * WIDTH_A + 2 * WIDTH_B], axis=-1)
    q = rms_norm(q.reshape(B, L, N_HEADS_B, HEAD_DIM), qn_g)
    k = rms_norm(k.reshape(B, L, N_HEADS_B, HEAD_DIM), kn_g)
    v = v.reshape(B, L, N_HEADS_B, HEAD_DIM).astype(jnp.float32)
    a_out, v_rows = chunk_mlp(ua, va, vn_g, w_s, b_s)
    b_out = attend(q, k, v).reshape(B, L, WIDTH_B)
    mixed = jnp.concatenate([rms_norm(a_out, on_a), rms_norm(b_out, on_b)], axis=-1) @ w_out
    x1 = x + g_m * mixed
    h2 = rms_norm(x1, norm_ffn_g) * (1.0 + sc_f) + sh_f
    y = x1 + g_f * (jnp.square(jax.nn.relu(h2 @ w1)) @ w2)
    return y.astype(x.dtype), k.astype(x.dtype), v.astype(x.dtype), v_rows.astype(x.dtype)


def setup_inputs(seed: int = 0) -> dict:
    key = jax.random.key(seed)
    ks = jax.random.split(key, 24)
    n_pages = PAST_LEN // PAGE_SIZE
    n_phys = (DEC_BATCH * n_pages * 5) // 4

    def nrm(k, shape, scale=1.0):
        return jax.random.normal(k, shape, jnp.float32) * scale

    page_table = jax.random.permutation(ks[4], n_phys)[: DEC_BATCH * n_pages]
    page_table = page_table.reshape(DEC_BATCH, n_pages).astype(jnp.int32)
    return {
        "x_prompt": nrm(ks[0], (BATCH, SEQ, D_MODEL)),
        "x_sample": nrm(ks[1], (DEC_BATCH, DEC_SEQ, D_MODEL)),
        "cache_k": nrm(ks[2], (DEPTH, n_phys, PAGE_SIZE, N_HEADS_B, HEAD_DIM)),
        "cache_v": nrm(ks[3], (DEPTH, n_phys, PAGE_SIZE, N_HEADS_B, HEAD_DIM)),
        "page_table": page_table,
        "c_prompt": nrm(ks[5], (BATCH, D_MODEL)),
        "c_sample": nrm(ks[6], (DEC_BATCH, D_MODEL)),
        "norm_mix_g": 1.0 + nrm(ks[7], (DEPTH, D_MODEL), 0.05),
        "norm_ffn_g": 1.0 + nrm(ks[8], (DEPTH, D_MODEL), 0.05),
        "w_ada": nrm(ks[9], (DEPTH, D_MODEL, 6 * D_MODEL), 0.5 * D_MODEL ** -0.5),
        "b_ada": nrm(ks[10], (DEPTH, 6 * D_MODEL), 0.01),
        "w_in": nrm(ks[11], (DEPTH, D_MODEL, IN_COLS), D_MODEL ** -0.5),
        "qn_g": 1.0 + nrm(ks[12], (DEPTH, HEAD_DIM), 0.05),
        "kn_g": 1.0 + nrm(ks[13], (DEPTH, HEAD_DIM), 0.05),
        "vn_g": 1.0 + nrm(ks[14], (DEPTH, HEAD_DIM), 0.05),
        "w_s": nrm(ks[15], (DEPTH, N_HEADS_A, CHUNK, CHUNK), CHUNK ** -0.5),
        "b_s": 1.0 + nrm(ks[16], (DEPTH, N_HEADS_A, CHUNK), 0.1),
        "on_a": 1.0 + nrm(ks[17], (DEPTH, WIDTH_A), 0.05),
        "on_b": 1.0 + nrm(ks[18], (DEPTH, WIDTH_B), 0.05),
        "w_out": nrm(ks[19], (DEPTH, MIX_WIDTH, D_MODEL), MIX_WIDTH ** -0.5),
        "w1": nrm(ks[20], (DEPTH, D_MODEL, D_FF), D_MODEL ** -0.5),
        "w2": nrm(ks[21], (DEPTH, D_FF, D_MODEL), D_FF ** -0.5),
    }


def reference(x_prompt, x_sample, cache_k, cache_v, page_table, c_prompt, c_sample,
              norm_mix_g, norm_ffn_g, w_ada, b_ada, w_in, qn_g, kn_g, vn_g, w_s, b_s,
              on_a, on_b, w_out, w1, w2):
    def attend_prompt(q, k, v):
        L = q.shape[1]
        return lax.map(lambda t: moba_seq(t[0], t[1], t[2], 0, L), (q, pad_blocks(k), pad_blocks(v)))

    def make_attend_sample(k_cache, v_cache):
        def attend(q, k, v):
            n_seq, L = q.shape[0], q.shape[1]
            past = page_table.shape[1] * k_cache.shape[1]
            k_past = k_cache[page_table].reshape(n_seq, past, N_HEADS_B, HEAD_DIM).astype(k.dtype)
            v_past = v_cache[page_table].reshape(n_seq, past, N_HEADS_B, HEAD_DIM).astype(v.dtype)
            k_all = pad_blocks(jnp.concatenate([k_past, k], axis=1))
            v_all = pad_blocks(jnp.concatenate([v_past, v], axis=1))
            return jax.vmap(lambda qq, kk, vv: moba_seq(qq, kk, vv, past, past + L))(q, k_all, v_all)
        return attend

    x_p, x_s = x_prompt, x_sample
    kp_l, vp_l, ks_l, vs_l, cv_l = [], [], [], [], []
    for l in range(DEPTH):
        p = (norm_mix_g[l], norm_ffn_g[l], w_ada[l], b_ada[l], w_in[l], qn_g[l], kn_g[l], vn_g[l],
             w_s[l], b_s[l], on_a[l], on_b[l], w_out[l], w1[l], w2[l])
        x_p, k_p, v_p, _ = layer(x_p, c_prompt, p, attend_prompt)
        x_s, k_s, v_s, cv_s = layer(x_s, c_sample, p, make_attend_sample(cache_k[l], cache_v[l]))
        kp_l.append(k_p); vp_l.append(v_p); ks_l.append(k_s); vs_l.append(v_s); cv_l.append(cv_s)
    return (x_p, x_s, jnp.stack(kp_l), jnp.stack(vp_l), jnp.stack(ks_l), jnp.stack(vs_l), jnp.stack(cv_l))
```

```python
import functools

import jax
import jax.numpy as jnp
from jax import lax
from jax.experimental import pallas as pl
from jax.experimental.pallas import tpu as pltpu

D_MODEL = 1024
HEAD_DIM = 64
N_HEADS = 8
WIDTH = N_HEADS * HEAD_DIM
IN_COLS = 5 * WIDTH
CHUNK = 128
MOBA_BLOCK = 256
MOBA_TOPK = 3
PAGE_SIZE = 128
PAGES_PER_BLOCK = MOBA_BLOCK // PAGE_SIZE
D_FF = 4 * D_MODEL
EPS = 1e-6
NEG = -1e30
SCALE = HEAD_DIM ** -0.5

LANES = 128
PAIR = LANES // HEAD_DIM
N_PAIRS = N_HEADS // PAIR
VMEM_LIMIT = 56 * 1024 * 1024

F32 = jnp.float32
BF16 = jnp.bfloat16


def _dot(a, b):
    return jnp.dot(a, b, preferred_element_type=F32)


def _rms_rows(x, g):
    return x * lax.rsqrt(jnp.mean(x * x, axis=-1, keepdims=True) + EPS) * g


def _head_rms(t, bd, g):
    t2 = (t * t).astype(BF16)
    half = bd.shape[0]
    ss = jnp.concatenate([_dot(t2[:, :half], bd), _dot(t2[:, half:], bd)], axis=-1)
    return t * lax.rsqrt(ss * (1.0 / HEAD_DIM) + EPS) * g


def _in_proj(x, sh, sc, g_mix, w_in):
    h = _rms_rows(x, g_mix) * (1.0 + sc) + sh
    return _dot(h.astype(BF16), w_in)


def _ada_kernel(c_ref, w_ref, b_ref, o_ref):
    s = jax.nn.silu(c_ref[...])
    o_ref[...] = jnp.dot(s, w_ref[...], preferred_element_type=F32,
                         precision=lax.Precision.HIGHEST) + b_ref[...]


def _ada(c_all, w_ada, b_ada):
    n, tn = c_all.shape[0], 512
    cols = w_ada.shape[1]
    return pl.pallas_call(
        _ada_kernel,
        grid=(cols // tn,),
        in_specs=[pl.BlockSpec((n, D_MODEL), lambda j: (0, 0)),
                  pl.BlockSpec((D_MODEL, tn), lambda j: (0, j)),
                  pl.BlockSpec((1, tn), lambda j: (0, j))],
        out_specs=pl.BlockSpec((n, tn), lambda j: (0, j)),
        out_shape=jax.ShapeDtypeStruct((n, cols), F32),
        compiler_params=pltpu.CompilerParams(dimension_semantics=("parallel",),
                                             vmem_limit_bytes=VMEM_LIMIT),
        name="ada_mod",
    )(c_all, w_ada, b_ada.reshape(1, cols))


def _prompt_in_kernel(x_ref, mod_ref, gmix_ref, win_ref, bd_ref, qg_ref, kg_ref, vg_ref,
                      ws_ref, bias_ref, ona_ref,
                      k_ref, v_ref, qt_ref, vt_ref, an_ref):
    tm = x_ref.shape[1]
    z = _in_proj(x_ref[0], mod_ref[0, 0:1, :], mod_ref[0, 1:2, :], gmix_ref[...], win_ref[...])
    bd = bd_ref[...]
    ua, va = z[:, 0:WIDTH], z[:, WIDTH:2 * WIDTH]
    q, k, v = z[:, 2 * WIDTH:3 * WIDTH], z[:, 3 * WIDTH:4 * WIDTH], z[:, 4 * WIDTH:5 * WIDTH]

    k_ref[0] = _head_rms(k, bd, kg_ref[...])
    v_ref[0] = v
    qt_ref[0, 0] = _head_rms(q, bd, qg_ref[...]).T
    vt_ref[0, 0] = v.T.astype(BF16)

    vn = _head_rms(jax.nn.gelu(va), bd, vg_ref[...]).astype(BF16)
    row = lax.broadcasted_iota(jnp.int32, (CHUNK, CHUNK), 0)
    col = lax.broadcasted_iota(jnp.int32, (CHUNK, CHUNK), 1)
    ws = [jnp.where(row >= col, ws_ref[h], 0.0).astype(BF16) for h in range(N_HEADS)]
    lane = lax.broadcasted_iota(jnp.int32, (CHUNK, LANES), 1)
    chunks = []
    for c in range(tm // CHUNK):
        pieces = []
        for p in range(N_PAIRS):
            vp = vn[c * CHUNK:(c + 1) * CHUNK, p * LANES:(p + 1) * LANES]
            pieces.append(jnp.where(lane < HEAD_DIM, _dot(ws[PAIR * p], vp), _dot(ws[PAIR * p + 1], vp)))
        chunks.append(jnp.concatenate(pieces, axis=-1) + bias_ref[...])
    mixed = jnp.concatenate(chunks, axis=0)
    an_ref[0] = _rms_rows(jax.nn.gelu(ua) * mixed, ona_ref[...]).astype(BF16)


def _prompt_in(x, mod, g_mix, w_in, bd, qg, kg, vg, w_s, bias, on_a, tm=MOBA_BLOCK):
    b, l, _ = x.shape
    nt = l // tm
    const2 = lambda i, j: (0, 0)
    return pl.pallas_call(
        _prompt_in_kernel,
        grid=(b, nt),
        in_specs=[pl.BlockSpec((1, tm, D_MODEL), lambda i, j: (i, j, 0)),
                  pl.BlockSpec((1, 6, D_MODEL), lambda i, j: (i, 0, 0)),
                  pl.BlockSpec((1, D_MODEL), const2),
                  pl.BlockSpec((D_MODEL, IN_COLS), const2),
                  pl.BlockSpec(bd.shape, const2),
                  pl.BlockSpec((1, WIDTH), const2),
                  pl.BlockSpec((1, WIDTH), const2),
                  pl.BlockSpec((1, WIDTH), const2),
                  pl.BlockSpec((N_HEADS, CHUNK, CHUNK), lambda i, j: (0, 0, 0)),
                  pl.BlockSpec((CHUNK, WIDTH), const2),
                  pl.BlockSpec((1, WIDTH), const2)],
        out_specs=[pl.BlockSpec((1, tm, WIDTH), lambda i, j: (i, j, 0)),
                   pl.BlockSpec((1, tm, WIDTH), lambda i, j: (i, j, 0)),
                   pl.BlockSpec((1, 1, WIDTH, tm), lambda i, j: (i, j, 0, 0)),
                   pl.BlockSpec((1, 1, WIDTH, tm), lambda i, j: (i, j, 0, 0)),
                   pl.BlockSpec((1, tm, WIDTH), lambda i, j: (i, j, 0))],
        out_shape=[jax.ShapeDtypeStruct((b, l, WIDTH), F32),
                   jax.ShapeDtypeStruct((b, l, WIDTH), F32),
                   jax.ShapeDtypeStruct((b, nt, WIDTH, tm), F32),
                   jax.ShapeDtypeStruct((b, nt, WIDTH, tm), BF16),
                   jax.ShapeDtypeStruct((b, l, WIDTH), BF16)],
        compiler_params=pltpu.CompilerParams(dimension_semantics=("parallel", "parallel"),
                                             vmem_limit_bytes=VMEM_LIMIT),
        name="prompt_in",
    )(x, mod, g_mix, w_in, bd, qg, kg, vg, w_s, bias, on_a)


def _prompt_attn_kernel(qt_ref, k_ref, vt_ref, o_ref, kbf_ref, kmean_ref, sel_ref):
    qb = pl.program_id(2)
    nb = kmean_ref.shape[0]
    blk = MOBA_BLOCK

    @pl.when(qb == 0)
    def _():
        kf = k_ref[0]
        kbf_ref[...] = kf.astype(BF16)
        for n in range(nb):
            kmean_ref[n:n + 1, :] = jnp.mean(kf[n * blk:(n + 1) * blk, :], axis=0, keepdims=True)

    qt = qt_ref[0, 0]
    drow = lax.broadcasted_iota(jnp.int32, qt.shape, 0)
    bid = lax.broadcasted_iota(jnp.int32, (nb, blk), 0)
    key_i = lax.broadcasted_iota(jnp.int32, (blk, blk), 0)
    qry_i = lax.broadcasted_iota(jnp.int32, (blk, blk), 1)
    kmean = kmean_ref[...]
    k_own = kbf_ref[pl.ds(pl.multiple_of(qb * blk, blk), blk), :]

    outs = []
    for hh in range(PAIR):
        lo = hh * HEAD_DIM
        qm = jnp.where((drow >= lo) & (drow < lo + HEAD_DIM), qt, 0.0)
        gate = jnp.dot(kmean, qm, preferred_element_type=F32, precision=lax.Precision.HIGHEST)
        cnt = jnp.zeros((nb, blk), jnp.int32)
        for m in range(nb):
            gm = gate[m:m + 1, :]
            beats = (gm > gate) | ((gm == gate) & (bid > m))
            cnt = cnt + jnp.where(beats & (qb > m), 1, 0)
        sel_ref[hh] = jnp.where((cnt < MOBA_TOPK) & (bid < qb), 1.0, 0.0)

        qs = (qm * SCALE).astype(BF16)
        s = jnp.where(key_i <= qry_i, _dot(k_own, qs), NEG)
        m0 = jnp.max(s, axis=0, keepdims=True)
        p = jnp.exp(s - m0)
        l0 = jnp.sum(p, axis=0, keepdims=True)
        acc0 = _dot(vt_ref[0, qb, lo:lo + HEAD_DIM, :], p.astype(BF16))

        def body(n, carry, hh=hh, lo=lo, qs=qs):
            m_i, l_i, acc = carry
            kb = kbf_ref[pl.ds(pl.multiple_of(n * blk, blk), blk), :]
            s = jnp.where(sel_ref[hh, pl.ds(n, 1), :] > 0.0, _dot(kb, qs), NEG)
            m_new = jnp.maximum(m_i, jnp.max(s, axis=0, keepdims=True))
            alpha = jnp.exp(m_i - m_new)
            p = jnp.exp(s - m_new)
            l_new = alpha * l_i + jnp.sum(p, axis=0, keepdims=True)
            acc = alpha * acc + _dot(vt_ref[0, n, lo:lo + HEAD_DIM, :], p.astype(BF16))
            return m_new, l_new, acc

        _, l_f, acc_f = lax.fori_loop(0, qb, body, (m0, l0, acc0))
        outs.append(acc_f / l_f)
    o_ref[0] = jnp.concatenate(outs, axis=0).T


def _prompt_attn(qt, k, vt):
    b, nb, _, blk = qt.shape
    l = nb * blk
    return pl.pallas_call(
        _prompt_attn_kernel,
        grid=(b, N_PAIRS, nb),
        in_specs=[pl.BlockSpec((1, 1, LANES, blk), lambda i, p, j: (i, j, p, 0)),
                  pl.BlockSpec((1, l, LANES), lambda i, p, j: (i, 0, p)),
                  pl.BlockSpec((1, nb, LANES, blk), lambda i, p, j: (i, 0, p, 0))],
        out_specs=pl.BlockSpec((1, blk, LANES), lambda i, p, j: (i, j, p)),
        out_shape=jax.ShapeDtypeStruct((b, l, WIDTH), F32),
        scratch_shapes=[pltpu.VMEM((l, LANES), BF16),
                        pltpu.VMEM((nb, LANES), F32),
                        pltpu.VMEM((PAIR, nb, blk), F32)],
        compiler_params=pltpu.CompilerParams(dimension_semantics=("parallel", "parallel", "arbitrary"),
                                             vmem_limit_bytes=VMEM_LIMIT),
        name="prompt_attn",
    )(qt, k, vt)


def _out_mlp(x, a_n, b_out, sh_f, sc_f, g_m, g_f, onb, gffn, wout_ref, w1_ref, w2_ref):
    b_n = _rms_rows(b_out, onb).astype(BF16)
    mixed = _dot(a_n, wout_ref[0:WIDTH, :]) + _dot(b_n, wout_ref[WIDTH:2 * WIDTH, :])
    x1 = x + g_m * mixed
    h2 = (_rms_rows(x1, gffn) * (1.0 + sc_f) + sh_f).astype(BF16)
    acc = jnp.zeros_like(x1)
    for c in range(D_FF // D_MODEL):
        t = jnp.maximum(_dot(h2, w1_ref[:, c * D_MODEL:(c + 1) * D_MODEL]), 0.0)
        acc = acc + _dot((t * t).astype(BF16), w2_ref[c * D_MODEL:(c + 1) * D_MODEL, :])
    return x1 + g_f * acc


def _prompt_out_kernel(x_ref, an_ref, b_ref, mod_ref, onb_ref, gffn_ref, wout_ref, w1_ref, w2_ref, y_ref):
    y_ref[0] = _out_mlp(x_ref[0], an_ref[0], b_ref[0],
                        mod_ref[0, 3:4, :], mod_ref[0, 4:5, :], mod_ref[0, 2:3, :], mod_ref[0, 5:6, :],
                        onb_ref[...], gffn_ref[...], wout_ref, w1_ref, w2_ref)


def _prompt_out(x, a_n, b_out, mod, on_b, g_ffn, w_out, w1, w2, tm=512):
    b, l, _ = x.shape
    const2 = lambda i, j: (0, 0)
    once = pl.Buffered(1)
    return pl.pallas_call(
        _prompt_out_kernel,
        grid=(b, l // tm),
        in_specs=[pl.BlockSpec((1, tm, D_MODEL), lambda i, j: (i, j, 0)),
                  pl.BlockSpec((1, tm, WIDTH), lambda i, j: (i, j, 0)),
                  pl.BlockSpec((1, tm, WIDTH), lambda i, j: (i, j, 0)),
                  pl.BlockSpec((1, 6, D_MODEL), lambda i, j: (i, 0, 0)),
                  pl.BlockSpec((1, WIDTH), const2),
                  pl.BlockSpec((1, D_MODEL), const2),
                  pl.BlockSpec((2 * WIDTH, D_MODEL), const2, pipeline_mode=once),
                  pl.BlockSpec((D_MODEL, D_FF), const2, pipeline_mode=once),
                  pl.BlockSpec((D_FF, D_MODEL), const2, pipeline_mode=once)],
        out_specs=pl.BlockSpec((1, tm, D_MODEL), lambda i, j: (i, j, 0)),
        out_shape=jax.ShapeDtypeStruct(x.shape, F32),
        compiler_params=pltpu.CompilerParams(dimension_semantics=("parallel", "parallel"),
                                             vmem_limit_bytes=VMEM_LIMIT),
        name="prompt_out",
    )(x, a_n, b_out, mod, on_b, g_ffn, w_out, w1, w2)


def _sample_out_kernel(x_ref, an_ref, b_ref, mod_ref, onb_ref, gffn_ref, wout_ref, w1_ref, w2_ref, y_ref):
    y_ref[...] = _out_mlp(x_ref[...], an_ref[...], b_ref[...],
                          mod_ref[3], mod_ref[4], mod_ref[2], mod_ref[5],
                          onb_ref[...], gffn_ref[...], wout_ref, w1_ref, w2_ref)


def _sample_out(x, a_n, b_out, mod_t, on_b, g_ffn, w_out, w1, w2):
    return pl.pallas_call(
        _sample_out_kernel,
        out_shape=jax.ShapeDtypeStruct(x.shape, F32),
        compiler_params=pltpu.CompilerParams(vmem_limit_bytes=VMEM_LIMIT),
        name="sample_out",
    )(x, a_n, b_out, mod_t, on_b, g_ffn, w_out, w1, w2)


def _sample_in_kernel(x_ref, mod_ref, gmix_ref, win_ref, bd_ref, qg_ref, kg_ref, vg_ref,
                      ws0_ref, bs0_ref, ona_ref,
                      q_ref, k_ref, v_ref, cv_ref, an_ref):
    z = _in_proj(x_ref[...], mod_ref[0], mod_ref[1], gmix_ref[...], win_ref[...])
    bd = bd_ref[...]
    ua, va = z[:, 0:WIDTH], z[:, WIDTH:2 * WIDTH]
    q, k, v = z[:, 2 * WIDTH:3 * WIDTH], z[:, 3 * WIDTH:4 * WIDTH], z[:, 4 * WIDTH:5 * WIDTH]
    q_ref[...] = _head_rms(q, bd, qg_ref[...])
    k_ref[...] = _head_rms(k, bd, kg_ref[...])
    v_ref[...] = v
    vn = _head_rms(jax.nn.gelu(va), bd, vg_ref[...])
    cv_ref[...] = vn
    mixed = ws0_ref[...] * vn + bs0_ref[...]
    an_ref[...] = _rms_rows(jax.nn.gelu(ua) * mixed, ona_ref[...]).astype(BF16)


def _sample_in(x, mod_t, g_mix, w_in, bd, qg, kg, vg, ws0, bs0, on_a):
    n = x.shape[0]
    row = jax.ShapeDtypeStruct((n, WIDTH), F32)
    return pl.pallas_call(
        _sample_in_kernel,
        out_shape=[row, row, row, row, jax.ShapeDtypeStruct((n, WIDTH), BF16)],
        compiler_params=pltpu.CompilerParams(vmem_limit_bytes=VMEM_LIMIT),
        name="sample_in",
    )(x, mod_t, g_mix, w_in, bd, qg, kg, vg, ws0, bs0, on_a)


PAGES_PER_STEP = 8


def _sample_gate_kernel(pt_ref, q_ref, e_ref, *refs):
    page_refs = refs[:PAGES_PER_STEP]
    sel_ref, kmean_ref = refs[PAGES_PER_STEP:]
    s, j = pl.program_id(0), pl.program_id(1)
    blocks_per_step = PAGES_PER_STEP // PAGES_PER_BLOCK
    for t in range(blocks_per_step):
        tot = sum(jnp.sum(page_refs[PAGES_PER_BLOCK * t + u][0], axis=0, keepdims=True)
                  for u in range(PAGES_PER_BLOCK))
        kmean_ref[pl.ds(j * blocks_per_step + t, 1), :] = tot * (1.0 / MOBA_BLOCK)

    @pl.when(j == pl.num_programs(1) - 1)
    def _():
        nb = kmean_ref.shape[0]
        prod = kmean_ref[...] * q_ref[pl.ds(s, 1), :]
        gate = jnp.dot(prod, e_ref[...], preferred_element_type=F32, precision=lax.Precision.HIGHEST)
        bidf = lax.broadcasted_iota(jnp.int32, gate.shape, 0).astype(F32)
        rank = lax.broadcasted_iota(jnp.int32, sel_ref.shape[1:], 0)
        out = jnp.zeros(sel_ref.shape[1:], F32)
        for r in range(MOBA_TOPK):
            mx = jnp.max(gate, axis=0, keepdims=True)
            idx = jnp.min(jnp.where(gate == mx, bidf, float(nb)), axis=0, keepdims=True)
            out = jnp.where(rank == r, idx, out)
            gate = jnp.where(bidf == idx, -jnp.inf, gate)
        sel_ref[0] = out.astype(jnp.int32)


def _sample_gate(page_table, q, cache_k2, e):
    n, n_pages = page_table.shape
    steps = n_pages // PAGES_PER_STEP
    nb = n_pages // PAGES_PER_BLOCK

    def page_map(t):
        return lambda s, j, pt: (pt[s * n_pages + j * PAGES_PER_STEP + t], 0, 0)

    grid_spec = pltpu.PrefetchScalarGridSpec(
        num_scalar_prefetch=1,
        grid=(n, steps),
        in_specs=[pl.BlockSpec((n, WIDTH), lambda s, j, pt: (0, 0)),
                  pl.BlockSpec(e.shape, lambda s, j, pt: (0, 0))]
                 + [pl.BlockSpec((1, PAGE_SIZE, WIDTH), page_map(t)) for t in range(PAGES_PER_STEP)],
        out_specs=pl.BlockSpec((1, 8, LANES), lambda s, j, pt: (s, 0, 0)),
        scratch_shapes=[pltpu.VMEM((nb, WIDTH), F32)])
    return pl.pallas_call(
        _sample_gate_kernel,
        grid_spec=grid_spec,
        out_shape=jax.ShapeDtypeStruct((n, 8, LANES), jnp.int32),
        compiler_params=pltpu.CompilerParams(dimension_semantics=("parallel", "arbitrary"),
                                             vmem_limit_bytes=VMEM_LIMIT),
        name="sample_gate",
    )(page_table.reshape(-1), q, e, *([cache_k2] * PAGES_PER_STEP))


SLABS = MOBA_TOPK * PAGES_PER_BLOCK
SLABS_PER_STEP = PAIR * SLABS


def _sample_attn_kernel(ids_ref, q_ref, kn_ref, vn_ref, *refs):
    k_refs = refs[:SLABS_PER_STEP]
    v_refs = refs[SLABS_PER_STEP:2 * SLABS_PER_STEP]
    o_ref = refs[2 * SLABS_PER_STEP]
    q, k_new, v_new = q_ref[0], kn_ref[0], vn_ref[0]
    lane = lax.broadcasted_iota(jnp.int32, q.shape, 1)
    out = jnp.zeros(q.shape, F32)
    for hh in range(PAIR):
        in_head = (lane >= hh * HEAD_DIM) & (lane < (hh + 1) * HEAD_DIM)
        qm = jnp.where(in_head, q, 0.0) * SCALE
        s_self = jnp.sum(qm * k_new, axis=-1, keepdims=True)
        logits = [jnp.sum(k_refs[hh * SLABS + t][0] * qm, axis=-1, keepdims=True) for t in range(SLABS)]
        m = s_self
        for s_t in logits:
            m = jnp.maximum(m, jnp.max(s_t, axis=0, keepdims=True))
        p_self = jnp.exp(s_self - m)
        l = p_self
        acc = p_self * v_new
        for t, s_t in enumerate(logits):
            p = jnp.exp(s_t - m)
            l = l + jnp.sum(p, axis=0, keepdims=True)
            acc = acc + jnp.sum(p * v_refs[hh * SLABS + t][0], axis=0, keepdims=True)
        out = jnp.where(in_head, acc / l, out)
    o_ref[0] = out


def _sample_attn(page_ids, q, k_new, v_new, cache_k2, cache_v2):
    n = q.shape[0]

    def slab_map(t):
        return lambda s, p, ids: (ids[(s * N_PAIRS + p) * SLABS_PER_STEP + t], 0, p)

    row_spec = pl.BlockSpec((1, 1, LANES), lambda s, p, ids: (s, 0, p))
    slab_specs = [pl.BlockSpec((1, PAGE_SIZE, LANES), slab_map(t)) for t in range(SLABS_PER_STEP)]
    grid_spec = pltpu.PrefetchScalarGridSpec(
        num_scalar_prefetch=1,
        grid=(n, N_PAIRS),
        in_specs=[row_spec, row_spec, row_spec] + slab_specs + slab_specs,
        out_specs=row_spec)
    out = pl.pallas_call(
        _sample_attn_kernel,
        grid_spec=grid_spec,
        out_shape=jax.ShapeDtypeStruct((n, 1, WIDTH), F32),
        compiler_params=pltpu.CompilerParams(dimension_semantics=("parallel", "parallel"),
                                             vmem_limit_bytes=VMEM_LIMIT),
        name="sample_attn",
    )(page_ids, q.reshape(n, 1, WIDTH), k_new.reshape(n, 1, WIDTH), v_new.reshape(n, 1, WIDTH),
      *([cache_k2] * SLABS_PER_STEP), *([cache_v2] * SLABS_PER_STEP))
    return out.reshape(n, WIDTH)


def _head_tile(g):
    return jnp.tile(g, N_HEADS).reshape(1, WIDTH)


def _layer(l, x_prompt, x_sample, cache_k, cache_v, page_table, c_prompt, c_sample,
           norm_mix_g, norm_ffn_g, w_ada, b_ada, w_in, qn_g, kn_g, vn_g, w_s, b_s,
           on_a, on_b, w_out, w1, w2):
    nb_, seq = x_prompt.shape[0], x_prompt.shape[1]
    ns = x_sample.shape[0]
    n_phys = cache_k.shape[1]

    w_in_b, w_out_b = w_in[l].astype(BF16), w_out[l].astype(BF16)
    w1_b, w2_b = w1[l].astype(BF16), w2[l].astype(BF16)
    g_mix, g_ffn = norm_mix_g[l].reshape(1, D_MODEL), norm_ffn_g[l].reshape(1, D_MODEL)
    qg, kg, vg = _head_tile(qn_g[l]), _head_tile(kn_g[l]), _head_tile(vn_g[l])
    ona, onb = on_a[l].reshape(1, WIDTH), on_b[l].reshape(1, WIDTH)
    head_of = jnp.arange(2 * LANES) // HEAD_DIM
    bd = (head_of[:, None] == head_of[None, :]).astype(BF16)
    e = (jnp.arange(WIDTH)[:, None] // HEAD_DIM == jnp.arange(LANES)[None, :]).astype(F32)
    bias = jnp.repeat(b_s[l].T, HEAD_DIM, axis=1)
    ws0 = jnp.repeat(w_s[l][:, 0, 0], HEAD_DIM).reshape(1, WIDTH)
    bs0 = jnp.repeat(b_s[l][:, 0], HEAD_DIM).reshape(1, WIDTH)

    mod = _ada(jnp.concatenate([c_prompt, c_sample], axis=0), w_ada[l], b_ada[l])
    mod_p = mod[:nb_].reshape(nb_, 6, D_MODEL)
    mod_s = mod[nb_:].reshape(ns, 6, D_MODEL).transpose(1, 0, 2)

    k_p, v_p, qt, vt, an_p = _prompt_in(x_prompt, mod_p, g_mix, w_in_b, bd, qg, kg, vg, w_s[l], bias, ona)
    b_p = _prompt_attn(qt, k_p, vt)
    y_p = _prompt_out(x_prompt, an_p, b_p, mod_p, onb, g_ffn, w_out_b, w1_b, w2_b)

    xs = x_sample.reshape(ns, D_MODEL)
    q_s, k_s, v_s, cv_s, an_s = _sample_in(xs, mod_s, g_mix, w_in_b, bd, qg, kg, vg, ws0, bs0, ona)
    ck2 = cache_k[l].reshape(n_phys, PAGE_SIZE, WIDTH)
    cv2 = cache_v[l].reshape(n_phys, PAGE_SIZE, WIDTH)
    sel = _sample_gate(page_table, q_s, ck2, e)[:, :MOBA_TOPK, :N_HEADS]
    logical = sel.transpose(0, 2, 1)[..., None] * PAGES_PER_BLOCK + jnp.arange(PAGES_PER_BLOCK)
    page_ids = jnp.take_along_axis(page_table, logical.reshape(ns, -1), axis=1)
    b_s_out = _sample_attn(page_ids.reshape(-1), q_s, k_s, v_s, ck2, cv2)
    y_s = _sample_out(xs, an_s, b_s_out, mod_s, onb, g_ffn, w_out_b, w1_b, w2_b)

    hshape = (N_HEADS, HEAD_DIM)
    return (y_p, y_s.reshape(x_sample.shape),
            k_p.reshape(nb_, seq, *hshape), v_p.reshape(nb_, seq, *hshape),
            k_s.reshape(ns, 1, *hshape), v_s.reshape(ns, 1, *hshape), cv_s.reshape(ns, 1, *hshape))


def kernel(x_prompt, x_sample, cache_k, cache_v, page_table, c_prompt, c_sample, norm_mix_g, norm_ffn_g,
           w_ada, b_ada, w_in, qn_g, kn_g, vn_g, w_s, b_s, on_a, on_b, w_out, w1, w2):
    depth = w_in.shape[0]
    x_p, x_s = x_prompt, x_sample
    kp, vp, ks, vs, cv = [], [], [], [], []
    for l in range(depth):
        x_p, x_s, k_p, v_p, k_s, v_s, cv_s = _layer(
            l, x_p, x_s, cache_k, cache_v, page_table, c_prompt, c_sample, norm_mix_g, norm_ffn_g,
            w_ada, b_ada, w_in, qn_g, kn_g, vn_g, w_s, b_s, on_a, on_b, w_out, w1, w2)
        kp.append(k_p); vp.append(v_p); ks.append(k_s); vs.append(v_s); cv.append(cv_s)
    return (x_p, x_s, jnp.stack(kp), jnp.stack(vp), jnp.stack(ks), jnp.stack(vs), jnp.stack(cv))
```

```python
import functools

import numpy as np
import jax
import jax.numpy as jnp
from jax import lax
from jax.experimental import pallas as pl
from jax.experimental.pallas import tpu as pltpu

D_MODEL = 1024
HEAD_DIM = 64
N_HEADS = 8
WIDTH = N_HEADS * HEAD_DIM
IN_COLS = 5 * WIDTH
CHUNK = 128
MOBA_BLOCK = 256
MOBA_TOPK = 3
PAGE_SIZE = 128
PAGES_PER_BLOCK = MOBA_BLOCK // PAGE_SIZE
D_FF = 4 * D_MODEL
EPS = 1e-6
NEG = -1e30
SCALE = HEAD_DIM ** -0.5
LOG2E = 1.4426950408889634

LANES = 128
SUBLANES = 8
PAIR = LANES // HEAD_DIM
N_PAIRS = N_HEADS // PAIR
LOGITS_AHEAD = 3
VMEM_LIMIT = 56 * 1024 * 1024

F32 = jnp.float32
BF16 = jnp.bfloat16


def _dot(a, b):
    return jnp.dot(a, b, preferred_element_type=F32)


def _rms_rows(x, g):
    return x * lax.rsqrt(jnp.mean(x * x, axis=-1, keepdims=True) + EPS) * g


def _head_rms(t, bd, g):
    t2 = (t * t).astype(BF16)
    half = bd.shape[0]
    ss = jnp.concatenate([_dot(t2[:, :half], bd), _dot(t2[:, half:], bd)], axis=-1)
    return t * lax.rsqrt(ss * (1.0 / HEAD_DIM) + EPS) * g


def _in_proj(x, sh, sc, g_mix, w_in):
    h = _rms_rows(x, g_mix) * (1.0 + sc) + sh
    return _dot(h.astype(BF16), w_in)


def _ada_kernel(c_ref, w_ref, b_ref, o_ref):
    s = jax.nn.silu(c_ref[...])
    o_ref[...] = jnp.dot(s, w_ref[...], preferred_element_type=F32,
                         precision=lax.Precision.HIGHEST) + b_ref[...]


def _ada(c_all, w_ada, b_ada):
    n, tn = c_all.shape[0], 512
    cols = w_ada.shape[1]
    return pl.pallas_call(
        _ada_kernel,
        grid=(cols // tn,),
        in_specs=[pl.BlockSpec((n, D_MODEL), lambda j: (0, 0)),
                  pl.BlockSpec((D_MODEL, tn), lambda j: (0, j)),
                  pl.BlockSpec((1, tn), lambda j: (0, j))],
        out_specs=pl.BlockSpec((n, tn), lambda j: (0, j)),
        out_shape=jax.ShapeDtypeStruct((n, cols), F32),
        compiler_params=pltpu.CompilerParams(dimension_semantics=("parallel",),
                                             vmem_limit_bytes=VMEM_LIMIT),
        name="ada_mod",
    )(c_all, w_ada, b_ada.reshape(1, cols))


def _prompt_in_kernel(x_ref, mod_ref, gmix_ref, win_ref, bd_ref, qg_ref, kg_ref, vg_ref,
                      ws_ref, bias_ref, ona_ref,
                      k_ref, v_ref, kbf_ref, km_ref, qt_ref, vt_ref, an_ref):
    tm = x_ref.shape[1]
    j = pl.program_id(1)
    z = _in_proj(x_ref[0], mod_ref[0, 0:1, :], mod_ref[0, 1:2, :], gmix_ref[...], win_ref[...])
    bd = bd_ref[...]
    ua, va = z[:, 0:WIDTH], z[:, WIDTH:2 * WIDTH]
    q, k, v = z[:, 2 * WIDTH:3 * WIDTH], z[:, 3 * WIDTH:4 * WIDTH], z[:, 4 * WIDTH:5 * WIDTH]

    k_n = _head_rms(k, bd, kg_ref[...])
    k_ref[0] = k_n
    kbf_ref[0] = k_n.astype(BF16)
    km_ref[0, pl.ds(j, 1), :] = jnp.mean(k_n, axis=0, keepdims=True)
    v_ref[0] = v
    qt_ref[0, 0] = _head_rms(q, bd, qg_ref[...]).T
    vt_ref[0, 0] = v.T.astype(BF16)

    vn = _head_rms(jax.nn.gelu(va), bd, vg_ref[...]).astype(BF16)
    row = lax.broadcasted_iota(jnp.int32, (CHUNK, CHUNK), 0)
    col = lax.broadcasted_iota(jnp.int32, (CHUNK, CHUNK), 1)
    ws = [jnp.where(row >= col, ws_ref[h], 0.0).astype(BF16) for h in range(N_HEADS)]
    lane = lax.broadcasted_iota(jnp.int32, (CHUNK, LANES), 1)
    chunks = []
    for c in range(tm // CHUNK):
        pieces = []
        for p in range(N_PAIRS):
            vp = vn[c * CHUNK:(c + 1) * CHUNK, p * LANES:(p + 1) * LANES]
            pieces.append(jnp.where(lane < HEAD_DIM, _dot(ws[PAIR * p], vp), _dot(ws[PAIR * p + 1], vp)))
        chunks.append(jnp.concatenate(pieces, axis=-1) + bias_ref[...])
    mixed = jnp.concatenate(chunks, axis=0)
    an_ref[0] = _rms_rows(jax.nn.gelu(ua) * mixed, ona_ref[...]).astype(BF16)


def _prompt_in(x, mod, g_mix, w_in, bd, qg, kg, vg, w_s, bias, on_a):
    b, l, _ = x.shape
    tm = MOBA_BLOCK
    nt = l // tm
    const2 = lambda i, j: (0, 0)
    rows = pl.BlockSpec((1, tm, WIDTH), lambda i, j: (i, j, 0))
    cols = pl.BlockSpec((1, 1, WIDTH, tm), lambda i, j: (i, j, 0, 0))
    return pl.pallas_call(
        _prompt_in_kernel,
        grid=(b, nt),
        in_specs=[pl.BlockSpec((1, tm, D_MODEL), lambda i, j: (i, j, 0)),
                  pl.BlockSpec((1, 6, D_MODEL), lambda i, j: (i, 0, 0)),
                  pl.BlockSpec((1, D_MODEL), const2),
                  pl.BlockSpec((D_MODEL, IN_COLS), const2),
                  pl.BlockSpec(bd.shape, const2),
                  pl.BlockSpec((1, WIDTH), const2),
                  pl.BlockSpec((1, WIDTH), const2),
                  pl.BlockSpec((1, WIDTH), const2),
                  pl.BlockSpec((N_HEADS, CHUNK, CHUNK), lambda i, j: (0, 0, 0)),
                  pl.BlockSpec((CHUNK, WIDTH), const2),
                  pl.BlockSpec((1, WIDTH), const2)],
        out_specs=[rows, rows, rows,
                   pl.BlockSpec((1, nt, WIDTH), lambda i, j: (i, 0, 0)),
                   cols, cols, rows],
        out_shape=[jax.ShapeDtypeStruct((b, l, WIDTH), F32),
                   jax.ShapeDtypeStruct((b, l, WIDTH), F32),
                   jax.ShapeDtypeStruct((b, l, WIDTH), BF16),
                   jax.ShapeDtypeStruct((b, nt, WIDTH), F32),
                   jax.ShapeDtypeStruct((b, nt, WIDTH, tm), F32),
                   jax.ShapeDtypeStruct((b, nt, WIDTH, tm), BF16),
                   jax.ShapeDtypeStruct((b, l, WIDTH), BF16)],
        compiler_params=pltpu.CompilerParams(dimension_semantics=("parallel", "arbitrary"),
                                             vmem_limit_bytes=VMEM_LIMIT),
        name="prompt_in",
    )(x, mod, g_mix, w_in, bd, qg, kg, vg, w_s, bias, on_a)


def _prompt_attn_kernel(qb_of, kb_of, qt_ref, k_ref, vt_ref, km_ref, o_ref,
                        qs_ref, sel_ref, m_ref, l_ref, acc_ref):
    t = pl.program_id(1)
    qb, kb = qb_of[t], kb_of[t]
    nb = km_ref.shape[1]
    blk = MOBA_BLOCK
    is_own = kb == qb

    @pl.when(is_own)
    def _():
        qt = qt_ref[0, 0]
        km = km_ref[0]
        lane = lax.broadcasted_iota(jnp.int32, (nb, LANES), 1)
        drow = lax.broadcasted_iota(jnp.int32, (LANES, blk), 0)
        bid = lax.broadcasted_iota(jnp.int32, (nb, blk), 0)
        for h in range(N_HEADS):
            p, hh = divmod(h, PAIR)
            lo = hh * HEAD_DIM
            qp = qt[p * LANES:(p + 1) * LANES]
            qs_ref[h] = jnp.where((drow >= lo) & (drow < lo + HEAD_DIM), qp * (SCALE * LOG2E), 0.0).astype(BF16)
            kmm = jnp.where((lane >= lo) & (lane < lo + HEAD_DIM), km[:, p * LANES:(p + 1) * LANES], 0.0)
            gate = jnp.dot(kmm, qp, preferred_element_type=F32, precision=lax.Precision.HIGHEST)
            cnt = jnp.zeros((nb, blk), jnp.int32)
            for m in range(nb):
                gm = gate[m:m + 1, :]
                beats = (gm > gate) | ((gm == gate) & (bid > m))
                cnt = cnt + jnp.where(beats & (qb > m), 1, 0)
            sel_ref[h] = jnp.where((cnt < MOBA_TOPK) & (bid < qb), 1.0, 0.0)

    def step(own):
        k = k_ref[0]
        if own:
            key_i = lax.broadcasted_iota(jnp.int32, (blk, blk), 0)
            qry_i = lax.broadcasted_iota(jnp.int32, (blk, blk), 1)
            causal = key_i <= qry_i

        def logits(h):
            p = h // PAIR
            return _dot(k[:, p * LANES:(p + 1) * LANES], qs_ref[h])

        ahead = [logits(h) for h in range(LOGITS_AHEAD)]
        for h in range(N_HEADS):
            rows = slice(h * HEAD_DIM, (h + 1) * HEAD_DIM)
            s = ahead.pop(0)
            if h + LOGITS_AHEAD < N_HEADS:
                ahead.append(logits(h + LOGITS_AHEAD))
            vt = vt_ref[0, 0, rows, :]
            if own:
                s = jnp.where(causal, s, NEG)
                m_new = jnp.max(s, axis=0, keepdims=True)
                pr = jnp.exp2(s - m_new)
                l_ref[h] = jnp.sum(pr, axis=0, keepdims=True)
                acc_ref[rows, :] = _dot(vt, pr.astype(BF16))
            else:
                taken = sel_ref[h, pl.ds(kb, 1), :] > 0.0
                m_i = m_ref[h]
                m_new = jnp.where(taken, jnp.maximum(m_i, jnp.max(s, axis=0, keepdims=True)), m_i)
                alpha = jnp.exp2(m_i - m_new)
                pr = jnp.exp2(s - jnp.where(taken, m_new, -NEG))
                l_ref[h] = alpha * l_ref[h] + jnp.sum(pr, axis=0, keepdims=True)
                acc_ref[rows, :] = alpha * acc_ref[rows, :] + _dot(vt, pr.astype(BF16))
            m_ref[h] = m_new

    pl.when(is_own)(functools.partial(step, True))
    pl.when(jnp.logical_not(is_own))(functools.partial(step, False))

    @pl.when((qb == 0) | (kb == qb - 1))
    def _():
        outs = [acc_ref[h * HEAD_DIM:(h + 1) * HEAD_DIM, :] / l_ref[h] for h in range(N_HEADS)]
        o_ref[0] = jnp.concatenate(outs, axis=0).T


def _prompt_attn(qt, kbf, vt, km):
    b, nb, _, blk = qt.shape
    l = nb * blk
    pairs = [(qb, kb) for qb in range(nb) for kb in [qb] + list(range(qb))]
    qb_of = jnp.asarray(np.array([p[0] for p in pairs], np.int32))
    kb_of = jnp.asarray(np.array([p[1] for p in pairs], np.int32))
    grid_spec = pltpu.PrefetchScalarGridSpec(
        num_scalar_prefetch=2,
        grid=(b, len(pairs)),
        in_specs=[pl.BlockSpec((1, 1, WIDTH, blk), lambda i, t, qo, ko: (i, qo[t], 0, 0)),
                  pl.BlockSpec((1, blk, WIDTH), lambda i, t, qo, ko: (i, ko[t], 0)),
                  pl.BlockSpec((1, 1, WIDTH, blk), lambda i, t, qo, ko: (i, ko[t], 0, 0)),
                  pl.BlockSpec((1, nb, WIDTH), lambda i, t, qo, ko: (i, 0, 0))],
        out_specs=pl.BlockSpec((1, blk, WIDTH), lambda i, t, qo, ko: (i, qo[t], 0)),
        scratch_shapes=[pltpu.VMEM((N_HEADS, LANES, blk), BF16),
                        pltpu.VMEM((N_HEADS, nb, blk), F32),
                        pltpu.VMEM((N_HEADS, 1, blk), F32),
                        pltpu.VMEM((N_HEADS, 1, blk), F32),
                        pltpu.VMEM((WIDTH, blk), F32)])
    return pl.pallas_call(
        _prompt_attn_kernel,
        grid_spec=grid_spec,
        out_shape=jax.ShapeDtypeStruct((b, l, WIDTH), F32),
        compiler_params=pltpu.CompilerParams(dimension_semantics=("parallel", "arbitrary"),
                                             vmem_limit_bytes=VMEM_LIMIT),
        name="prompt_attn",
    )(qb_of, kb_of, qt, kbf, vt, km)


def _out_mlp(x, a_n, b_out, sh_f, sc_f, g_m, g_f, onb, gffn, wout_ref, w1_ref, w2_ref):
    b_n = _rms_rows(b_out, onb).astype(BF16)
    mixed = _dot(a_n, wout_ref[0:WIDTH, :]) + _dot(b_n, wout_ref[WIDTH:2 * WIDTH, :])
    x1 = x + g_m * mixed
    h2 = (_rms_rows(x1, gffn) * (1.0 + sc_f) + sh_f).astype(BF16)
    acc = jnp.zeros_like(x1)
    for c in range(D_FF // D_MODEL):
        t = jnp.maximum(_dot(h2, w1_ref[:, c * D_MODEL:(c + 1) * D_MODEL]), 0.0)
        acc = acc + _dot((t * t).astype(BF16), w2_ref[c * D_MODEL:(c + 1) * D_MODEL, :])
    return x1 + g_f * acc


def _prompt_out_kernel(x_ref, an_ref, b_ref, mod_ref, onb_ref, gffn_ref, wout_ref, w1_ref, w2_ref, y_ref):
    y_ref[0] = _out_mlp(x_ref[0], an_ref[0], b_ref[0],
                        mod_ref[0, 3:4, :], mod_ref[0, 4:5, :], mod_ref[0, 2:3, :], mod_ref[0, 5:6, :],
                        onb_ref[...], gffn_ref[...], wout_ref, w1_ref, w2_ref)


def _prompt_out(x, a_n, b_out, mod, on_b, g_ffn, w_out, w1, w2, tm=512):
    b, l, _ = x.shape
    const2 = lambda i, j: (0, 0)
    once = pl.Buffered(1)
    return pl.pallas_call(
        _prompt_out_kernel,
        grid=(b, l // tm),
        in_specs=[pl.BlockSpec((1, tm, D_MODEL), lambda i, j: (i, j, 0)),
                  pl.BlockSpec((1, tm, WIDTH), lambda i, j: (i, j, 0)),
                  pl.BlockSpec((1, tm, WIDTH), lambda i, j: (i, j, 0)),
                  pl.BlockSpec((1, 6, D_MODEL), lambda i, j: (i, 0, 0)),
                  pl.BlockSpec((1, WIDTH), const2),
                  pl.BlockSpec((1, D_MODEL), const2),
                  pl.BlockSpec((2 * WIDTH, D_MODEL), const2, pipeline_mode=once),
                  pl.BlockSpec((D_MODEL, D_FF), const2, pipeline_mode=once),
                  pl.BlockSpec((D_FF, D_MODEL), const2, pipeline_mode=once)],
        out_specs=pl.BlockSpec((1, tm, D_MODEL), lambda i, j: (i, j, 0)),
        out_shape=jax.ShapeDtypeStruct(x.shape, F32),
        compiler_params=pltpu.CompilerParams(dimension_semantics=("parallel", "parallel"),
                                             vmem_limit_bytes=VMEM_LIMIT),
        name="prompt_out",
    )(x, a_n, b_out, mod, on_b, g_ffn, w_out, w1, w2)


def _sample_out_kernel(x_ref, an_ref, b_ref, mod_ref, onb_ref, gffn_ref, wout_ref, w1_ref, w2_ref, y_ref):
    y_ref[...] = _out_mlp(x_ref[...], an_ref[...], b_ref[...],
                          mod_ref[3], mod_ref[4], mod_ref[2], mod_ref[5],
                          onb_ref[...], gffn_ref[...], wout_ref, w1_ref, w2_ref)


def _sample_out(x, a_n, b_out, mod_t, on_b, g_ffn, w_out, w1, w2):
    return pl.pallas_call(
        _sample_out_kernel,
        out_shape=jax.ShapeDtypeStruct(x.shape, F32),
        compiler_params=pltpu.CompilerParams(vmem_limit_bytes=VMEM_LIMIT),
        name="sample_out",
    )(x, a_n, b_out, mod_t, on_b, g_ffn, w_out, w1, w2)


def _sample_in_kernel(x_ref, mod_ref, gmix_ref, win_ref, bd_ref, qg_ref, kg_ref, vg_ref,
                      ws0_ref, bs0_ref, ona_ref,
                      q_ref, k_ref, v_ref, cv_ref, an_ref):
    z = _in_proj(x_ref[...], mod_ref[0], mod_ref[1], gmix_ref[...], win_ref[...])
    bd = bd_ref[...]
    ua, va = z[:, 0:WIDTH], z[:, WIDTH:2 * WIDTH]
    q, k, v = z[:, 2 * WIDTH:3 * WIDTH], z[:, 3 * WIDTH:4 * WIDTH], z[:, 4 * WIDTH:5 * WIDTH]
    q_ref[...] = _head_rms(q, bd, qg_ref[...])
    k_ref[...] = _head_rms(k, bd, kg_ref[...])
    v_ref[...] = v
    vn = _head_rms(jax.nn.gelu(va), bd, vg_ref[...])
    cv_ref[...] = vn
    mixed = ws0_ref[...] * vn + bs0_ref[...]
    an_ref[...] = _rms_rows(jax.nn.gelu(ua) * mixed, ona_ref[...]).astype(BF16)


def _sample_in(x, mod_t, g_mix, w_in, bd, qg, kg, vg, ws0, bs0, on_a):
    n = x.shape[0]
    row = jax.ShapeDtypeStruct((n, WIDTH), F32)
    return pl.pallas_call(
        _sample_in_kernel,
        out_shape=[row, row, row, row, jax.ShapeDtypeStruct((n, WIDTH), BF16)],
        compiler_params=pltpu.CompilerParams(vmem_limit_bytes=VMEM_LIMIT),
        name="sample_in",
    )(x, mod_t, g_mix, w_in, bd, qg, kg, vg, ws0, bs0, on_a)


def _row_as_columns(row):
    return jnp.broadcast_to(row, (LANES, WIDTH)).T


GATE_PAGES = 16
GATE_BLOCKS = GATE_PAGES // PAGES_PER_BLOCK
DIM_GROUPS = HEAD_DIM // SUBLANES


def _sample_gate_kernel(pt_ref, q_ref, *refs):
    page_refs = refs[:GATE_PAGES]
    sel_ref, qcol_ref, g_ref = refs[GATE_PAGES:]
    s, j = pl.program_id(0), pl.program_id(1)
    nb = g_ref.shape[0]

    @pl.when(j == 0)
    def _():
        qt = _row_as_columns(q_ref[pl.ds(s, 1), :])
        for h in range(N_HEADS):
            qcol_ref[h] = qt[h * HEAD_DIM:(h + 1) * HEAD_DIM, :]

    for t in range(GATE_BLOCKS):
        heads = []
        for h in range(N_HEADS):
            tot = None
            for u in range(PAGES_PER_BLOCK):
                for g in range(DIM_GROUPS):
                    rows = slice(g * SUBLANES, (g + 1) * SUBLANES)
                    term = page_refs[PAGES_PER_BLOCK * t + u][0, 0, h, rows, :] * qcol_ref[h, rows, :]
                    tot = term if tot is None else tot + term
            heads.append(tot)
        g_ref[j * GATE_BLOCKS + t] = jnp.stack(heads, axis=0)

    @pl.when(j == pl.num_programs(1) - 1)
    def _():
        part = jnp.sum(g_ref[...], axis=3, keepdims=True)
        gate = jnp.sum(part, axis=2, keepdims=True) * (1.0 / MOBA_BLOCK)
        bidf = lax.broadcasted_iota(jnp.int32, gate.shape, 0).astype(F32)
        picks = []
        for _ in range(MOBA_TOPK):
            mx = jnp.max(gate, axis=0, keepdims=True)
            idx = jnp.min(jnp.where(gate == mx, bidf, float(nb)), axis=0, keepdims=True)
            picks.append(idx)
            gate = jnp.where(bidf == idx, -jnp.inf, gate)
        sel = jnp.concatenate(picks, axis=0)
        sel_ref[0] = jnp.broadcast_to(sel, sel_ref.shape[1:]).astype(jnp.int32)


def _sample_gate(l, page_table, q, cache_kt):
    n, n_pages = page_table.shape
    steps = n_pages // GATE_PAGES
    nb = n_pages // PAGES_PER_BLOCK

    def page_map(t):
        return lambda s, j, pt: (l, pt[s * n_pages + j * GATE_PAGES + t], 0, 0, 0)

    page_block = (1, 1, N_HEADS, HEAD_DIM, PAGE_SIZE)
    sel_block = (MOBA_TOPK, N_HEADS, SUBLANES, LANES)
    grid_spec = pltpu.PrefetchScalarGridSpec(
        num_scalar_prefetch=1,
        grid=(n, steps),
        in_specs=[pl.BlockSpec((n, WIDTH), lambda s, j, pt: (0, 0))]
                 + [pl.BlockSpec(page_block, page_map(t)) for t in range(GATE_PAGES)],
        out_specs=pl.BlockSpec((1,) + sel_block, lambda s, j, pt: (s, 0, 0, 0, 0)),
        scratch_shapes=[pltpu.VMEM((N_HEADS, HEAD_DIM, LANES), F32),
                        pltpu.VMEM((nb, N_HEADS, SUBLANES, LANES), F32)])
    sel = pl.pallas_call(
        _sample_gate_kernel,
        grid_spec=grid_spec,
        out_shape=jax.ShapeDtypeStruct((n,) + sel_block, jnp.int32),
        compiler_params=pltpu.CompilerParams(dimension_semantics=("parallel", "arbitrary"),
                                             vmem_limit_bytes=VMEM_LIMIT),
        name="sample_gate",
    )(page_table.reshape(-1), q, *([cache_kt] * GATE_PAGES))
    return sel[:, :, :, 0, 0]


SLABS = MOBA_TOPK * PAGES_PER_BLOCK
N_SLABS = N_HEADS * SLABS


def _sample_attn_kernel(l, ids_ref, q_ref, kn_ref, vn_ref, ck_ref, cv_ref, o_ref, kbuf, vbuf, sem):
    s = pl.program_id(0)
    n = pl.num_programs(0)
    slot = lax.rem(s, 2)

    def copies(seq, slot_):
        out = []
        for i in range(N_SLABS):
            page = ids_ref[seq * N_SLABS + i]
            h = i // SLABS
            out.append(pltpu.make_async_copy(ck_ref.at[l, page, h], kbuf.at[slot_, i], sem.at[0, slot_]))
            out.append(pltpu.make_async_copy(cv_ref.at[l, page, h], vbuf.at[slot_, i], sem.at[1, slot_]))
        return out

    @pl.when(s == 0)
    def _():
        for c in copies(0, 0):
            c.start()

    @pl.when(s + 1 < n)
    def _():
        for c in copies(s + 1, 1 - slot):
            c.start()

    for c in copies(s, slot):
        c.wait()

    qt = _row_as_columns(q_ref[pl.ds(s, 1), :] * SCALE)
    knt = _row_as_columns(kn_ref[pl.ds(s, 1), :])
    lane_head = lax.broadcasted_iota(jnp.int32, (1, WIDTH), 1) // HEAD_DIM
    pself_row = jnp.zeros((1, WIDTH), F32)
    l_row = jnp.zeros((1, WIDTH), F32)
    accs = []
    for h in range(N_HEADS):
        rows = slice(h * HEAD_DIM, (h + 1) * HEAD_DIM)
        qh = qt[rows]
        s_self = jnp.sum(qh * knt[rows], axis=0, keepdims=True)
        logits = [jnp.sum(kbuf[slot, h * SLABS + t] * qh, axis=0, keepdims=True) for t in range(SLABS)]
        mx = logits[0]
        for s_t in logits[1:]:
            mx = jnp.maximum(mx, s_t)
        m = jnp.maximum(s_self, jnp.max(mx, axis=1, keepdims=True))
        p_self = jnp.exp(s_self - m)
        l_h = p_self
        acc = jnp.zeros((HEAD_DIM, LANES), F32)
        for t, s_t in enumerate(logits):
            p = jnp.exp(s_t - m)
            l_h = l_h + jnp.sum(p, axis=1, keepdims=True)
            acc = acc + vbuf[slot, h * SLABS + t] * p
        accs.append(acc)
        pself_row = jnp.where(lane_head == h, p_self[:, 0:1], pself_row)
        l_row = jnp.where(lane_head == h, l_h[:, 0:1], l_row)
    pv = jnp.sum(jnp.concatenate(accs, axis=0).T, axis=0, keepdims=True)
    o_ref[pl.ds(s, 1), :] = (pv + pself_row * vn_ref[pl.ds(s, 1), :]) / l_row


def _sample_attn(l, page_ids, q, k_new, v_new, cache_kt, cache_vt):
    n = q.shape[0]
    full = pl.BlockSpec((n, WIDTH), lambda s, ids: (0, 0))
    grid_spec = pltpu.PrefetchScalarGridSpec(
        num_scalar_prefetch=1,
        grid=(n,),
        in_specs=[full, full, full, pl.BlockSpec(memory_space=pl.ANY), pl.BlockSpec(memory_space=pl.ANY)],
        out_specs=full,
        scratch_shapes=[pltpu.VMEM((2, N_SLABS, HEAD_DIM, PAGE_SIZE), F32),
                        pltpu.VMEM((2, N_SLABS, HEAD_DIM, PAGE_SIZE), F32),
                        pltpu.SemaphoreType.DMA((2, 2))])
    return pl.pallas_call(
        functools.partial(_sample_attn_kernel, l),
        grid_spec=grid_spec,
        out_shape=jax.ShapeDtypeStruct((n, WIDTH), F32),
        compiler_params=pltpu.CompilerParams(dimension_semantics=("arbitrary",),
                                             vmem_limit_bytes=VMEM_LIMIT),
        name="sample_attn",
    )(page_ids, q, k_new, v_new, cache_kt, cache_vt)


def _head_tile(g):
    return jnp.tile(g, N_HEADS).reshape(1, WIDTH)


def _layer(l, x_prompt, x_sample, cache_kt, cache_vt, page_table, c_prompt, c_sample,
           norm_mix_g, norm_ffn_g, w_ada, b_ada, w_in, qn_g, kn_g, vn_g, w_s, b_s,
           on_a, on_b, w_out, w1, w2):
    nb_, seq = x_prompt.shape[0], x_prompt.shape[1]
    ns = x_sample.shape[0]

    w_in_b, w_out_b = w_in[l].astype(BF16), w_out[l].astype(BF16)
    w1_b, w2_b = w1[l].astype(BF16), w2[l].astype(BF16)
    g_mix, g_ffn = norm_mix_g[l].reshape(1, D_MODEL), norm_ffn_g[l].reshape(1, D_MODEL)
    qg, kg, vg = _head_tile(qn_g[l]), _head_tile(kn_g[l]), _head_tile(vn_g[l])
    ona, onb = on_a[l].reshape(1, WIDTH), on_b[l].reshape(1, WIDTH)
    head_of = jnp.arange(2 * LANES) // HEAD_DIM
    bd = (head_of[:, None] == head_of[None, :]).astype(BF16)
    bias = jnp.repeat(b_s[l].T, HEAD_DIM, axis=1)
    ws0 = jnp.repeat(w_s[l][:, 0, 0], HEAD_DIM).reshape(1, WIDTH)
    bs0 = jnp.repeat(b_s[l][:, 0], HEAD_DIM).reshape(1, WIDTH)

    mod = _ada(jnp.concatenate([c_prompt, c_sample], axis=0), w_ada[l], b_ada[l])
    mod_p = mod[:nb_].reshape(nb_, 6, D_MODEL)
    mod_s = mod[nb_:].reshape(ns, 6, D_MODEL).transpose(1, 0, 2)

    k_p, v_p, kbf, km, qt, vt, an_p = _prompt_in(x_prompt, mod_p, g_mix, w_in_b, bd, qg, kg, vg,
                                                 w_s[l], bias, ona)
    b_p = _prompt_attn(qt, kbf, vt, km)
    y_p = _prompt_out(x_prompt, an_p, b_p, mod_p, onb, g_ffn, w_out_b, w1_b, w2_b)

    xs = x_sample.reshape(ns, D_MODEL)
    q_s, k_s, v_s, cv_s, an_s = _sample_in(xs, mod_s, g_mix, w_in_b, bd, qg, kg, vg, ws0, bs0, ona)
    sel = _sample_gate(l, page_table, q_s, cache_kt)
    logical = sel.transpose(0, 2, 1)[..., None] * PAGES_PER_BLOCK + jnp.arange(PAGES_PER_BLOCK)
    page_ids = jnp.take_along_axis(page_table, logical.reshape(ns, -1), axis=1)
    b_s_out = _sample_attn(l, page_ids.reshape(-1), q_s, k_s, v_s, cache_kt, cache_vt)
    y_s = _sample_out(xs, an_s, b_s_out, mod_s, onb, g_ffn, w_out_b, w1_b, w2_b)

    hshape = (N_HEADS, HEAD_DIM)
    return (y_p, y_s.reshape(x_sample.shape),
            k_p.reshape(nb_, seq, *hshape), v_p.reshape(nb_, seq, *hshape),
            k_s.reshape(ns, 1, *hshape), v_s.reshape(ns, 1, *hshape), cv_s.reshape(ns, 1, *hshape))


def kernel(x_prompt, x_sample, cache_k, cache_v, page_table, c_prompt, c_sample, norm_mix_g, norm_ffn_g,
           w_ada, b_ada, w_in, qn_g, kn_g, vn_g, w_s, b_s, on_a, on_b, w_out, w1, w2):
    depth = w_in.shape[0]
    cache_kt = jnp.transpose(cache_k, (0, 1, 3, 4, 2))
    cache_vt = jnp.transpose(cache_v, (0, 1, 3, 4, 2))
    x_p, x_s = x_prompt, x_sample
    kp, vp, ks, vs, cv = [], [], [], [], []
    for l in range(depth):
        x_p, x_s, k_p, v_p, k_s, v_s, cv_s = _layer(
            l, x_p, x_s, cache_kt, cache_vt, page_table, c_prompt, c_sample, norm_mix_g, norm_ffn_g,
            w_ada, b_ada, w_in, qn_g, kn_g, vn_g, w_s, b_s, on_a, on_b, w_out, w1, w2)
        kp.append(k_p); vp.append(v_p); ks.append(k_s); vs.append(v_s); cv.append(cv_s)
    return (x_p, x_s, jnp.stack(kp), jnp.stack(vp), jnp.stack(ks), jnp.stack(vs), jnp.stack(cv))
```

```python
import functools

import jax
import jax.numpy as jnp
from jax import lax
from jax.experimental import pallas as pl
from jax.experimental.pallas import tpu as pltpu

D_MODEL = 1024
HEAD_DIM = 64
N_HEADS = 8
WIDTH = N_HEADS * HEAD_DIM
IN_COLS = 5 * WIDTH
CHUNK = 128
MOBA_BLOCK = 256
MOBA_TOPK = 3
PAGE_SIZE = 128
PAGES_PER_BLOCK = MOBA_BLOCK // PAGE_SIZE
D_FF = 4 * D_MODEL
EPS = 1e-6
NEG = -1e30
SCALE = HEAD_DIM ** -0.5
LOG2E = 1.4426950408889634

LANES = 128
SUBLANES = 8
PAIR = LANES // HEAD_DIM
N_PAIRS = N_HEADS // PAIR
LOGITS_AHEAD = 3
VMEM_LIMIT = 56 * 1024 * 1024

F32 = jnp.float32
BF16 = jnp.bfloat16


def _dot(a, b):
    return jnp.dot(a, b, preferred_element_type=F32)


def _rms_rows(x, g):
    return x * lax.rsqrt(jnp.mean(x * x, axis=-1, keepdims=True) + EPS) * g


def _head_rms(t, bd, g):
    t2 = (t * t).astype(BF16)
    half = bd.shape[0]
    ss = jnp.concatenate([_dot(t2[:, :half], bd), _dot(t2[:, half:], bd)], axis=-1)
    return t * lax.rsqrt(ss * (1.0 / HEAD_DIM) + EPS) * g


def _in_proj(x, sh, sc, g_mix, w_in):
    h = _rms_rows(x, g_mix) * (1.0 + sc) + sh
    return _dot(h.astype(BF16), w_in)


def _ada_kernel(c_ref, w_ref, b_ref, o_ref):
    s = jax.nn.silu(c_ref[...])
    o_ref[...] = jnp.dot(s, w_ref[...], preferred_element_type=F32,
                         precision=lax.Precision.HIGHEST) + b_ref[...]


def _ada(c_all, w_ada, b_ada):
    n, tn = c_all.shape[0], 512
    cols = w_ada.shape[1]
    return pl.pallas_call(
        _ada_kernel,
        grid=(cols // tn,),
        in_specs=[pl.BlockSpec((n, D_MODEL), lambda j: (0, 0)),
                  pl.BlockSpec((D_MODEL, tn), lambda j: (0, j)),
                  pl.BlockSpec((1, tn), lambda j: (0, j))],
        out_specs=pl.BlockSpec((n, tn), lambda j: (0, j)),
        out_shape=jax.ShapeDtypeStruct((n, cols), F32),
        compiler_params=pltpu.CompilerParams(dimension_semantics=("parallel",),
                                             vmem_limit_bytes=VMEM_LIMIT),
        name="ada_mod",
    )(c_all, w_ada, b_ada.reshape(1, cols))


def _prompt_in_kernel(x_ref, mod_ref, gmix_ref, win_ref, bd_ref, qg_ref, kg_ref, vg_ref,
                      ws_ref, bias_ref, ona_ref,
                      k_ref, v_ref, kbf_ref, km_ref, qt_ref, vt_ref, an_ref):
    tm = x_ref.shape[1]
    j = pl.program_id(1)
    z = _in_proj(x_ref[0], mod_ref[0, 0:1, :], mod_ref[0, 1:2, :], gmix_ref[...], win_ref[...])
    bd = bd_ref[...]
    ua, va = z[:, 0:WIDTH], z[:, WIDTH:2 * WIDTH]
    q, k, v = z[:, 2 * WIDTH:3 * WIDTH], z[:, 3 * WIDTH:4 * WIDTH], z[:, 4 * WIDTH:5 * WIDTH]

    k_n = _head_rms(k, bd, kg_ref[...])
    k_ref[0] = k_n
    kbf_ref[0] = k_n.astype(BF16)
    km_ref[0, pl.ds(j, 1), :] = jnp.mean(k_n, axis=0, keepdims=True)
    v_ref[0] = v
    qt_ref[0, 0] = _head_rms(q, bd, qg_ref[...]).T
    vt_ref[0, 0] = v.T.astype(BF16)

    vn = _head_rms(jax.nn.gelu(va), bd, vg_ref[...]).astype(BF16)
    row = lax.broadcasted_iota(jnp.int32, (CHUNK, CHUNK), 0)
    col = lax.broadcasted_iota(jnp.int32, (CHUNK, CHUNK), 1)
    ws = [jnp.where(row >= col, ws_ref[h], 0.0).astype(BF16) for h in range(N_HEADS)]
    lane = lax.broadcasted_iota(jnp.int32, (CHUNK, LANES), 1)
    chunks = []
    for c in range(tm // CHUNK):
        pieces = []
        for p in range(N_PAIRS):
            vp = vn[c * CHUNK:(c + 1) * CHUNK, p * LANES:(p + 1) * LANES]
            pieces.append(jnp.where(lane < HEAD_DIM, _dot(ws[PAIR * p], vp), _dot(ws[PAIR * p + 1], vp)))
        chunks.append(jnp.concatenate(pieces, axis=-1) + bias_ref[...])
    mixed = jnp.concatenate(chunks, axis=0)
    an_ref[0] = _rms_rows(jax.nn.gelu(ua) * mixed, ona_ref[...]).astype(BF16)


def _prompt_in(x, mod, g_mix, w_in, bd, qg, kg, vg, w_s, bias, on_a):
    b, l, _ = x.shape
    tm = MOBA_BLOCK
    nt = l // tm
    const2 = lambda i, j: (0, 0)
    rows = pl.BlockSpec((1, tm, WIDTH), lambda i, j: (i, j, 0))
    cols = pl.BlockSpec((1, 1, WIDTH, tm), lambda i, j: (i, j, 0, 0))
    return pl.pallas_call(
        _prompt_in_kernel,
        grid=(b, nt),
        in_specs=[pl.BlockSpec((1, tm, D_MODEL), lambda i, j: (i, j, 0)),
                  pl.BlockSpec((1, 6, D_MODEL), lambda i, j: (i, 0, 0)),
                  pl.BlockSpec((1, D_MODEL), const2),
                  pl.BlockSpec((D_MODEL, IN_COLS), const2),
                  pl.BlockSpec(bd.shape, const2),
                  pl.BlockSpec((1, WIDTH), const2),
                  pl.BlockSpec((1, WIDTH), const2),
                  pl.BlockSpec((1, WIDTH), const2),
                  pl.BlockSpec((N_HEADS, CHUNK, CHUNK), lambda i, j: (0, 0, 0)),
                  pl.BlockSpec((CHUNK, WIDTH), const2),
                  pl.BlockSpec((1, WIDTH), const2)],
        out_specs=[rows, rows, rows,
                   pl.BlockSpec((1, nt, WIDTH), lambda i, j: (i, 0, 0)),
                   cols, cols, rows],
        out_shape=[jax.ShapeDtypeStruct((b, l, WIDTH), F32),
                   jax.ShapeDtypeStruct((b, l, WIDTH), F32),
                   jax.ShapeDtypeStruct((b, l, WIDTH), BF16),
                   jax.ShapeDtypeStruct((b, nt, WIDTH), F32),
                   jax.ShapeDtypeStruct((b, nt, WIDTH, tm), F32),
                   jax.ShapeDtypeStruct((b, nt, WIDTH, tm), BF16),
                   jax.ShapeDtypeStruct((b, l, WIDTH), BF16)],
        compiler_params=pltpu.CompilerParams(dimension_semantics=("parallel", "arbitrary"),
                                             vmem_limit_bytes=VMEM_LIMIT),
        name="prompt_in",
    )(x, mod, g_mix, w_in, bd, qg, kg, vg, w_s, bias, on_a)


def _prompt_attn_kernel(qt_ref, k_ref, vt_ref, km_ref, o_ref, qs_ref, sel_ref, m_ref, l_ref, acc_ref):
    qb = pl.program_id(1)
    nb = km_ref.shape[1]
    blk = MOBA_BLOCK

    qt = qt_ref[0, 0]
    km = km_ref[0]
    lane = lax.broadcasted_iota(jnp.int32, (nb, LANES), 1)
    drow = lax.broadcasted_iota(jnp.int32, (LANES, blk), 0)
    bid = lax.broadcasted_iota(jnp.int32, (nb, blk), 0)
    for h in range(N_HEADS):
        p, hh = divmod(h, PAIR)
        lo = hh * HEAD_DIM
        qp = qt[p * LANES:(p + 1) * LANES]
        qs_ref[h] = jnp.where((drow >= lo) & (drow < lo + HEAD_DIM), qp * (SCALE * LOG2E), 0.0).astype(BF16)
        kmm = jnp.where((lane >= lo) & (lane < lo + HEAD_DIM), km[:, p * LANES:(p + 1) * LANES], 0.0)
        gate = jnp.dot(kmm, qp, preferred_element_type=F32, precision=lax.Precision.HIGHEST)
        cnt = jnp.zeros((nb, blk), jnp.int32)
        for m in range(nb):
            gm = gate[m:m + 1, :]
            beats = (gm > gate) | ((gm == gate) & (bid > m))
            cnt = cnt + jnp.where(beats & (qb > m), 1, 0)
        sel_ref[h] = jnp.where((cnt < MOBA_TOPK) & (bid < qb), 1.0, 0.0)

    def step(own, kb):
        k = k_ref[0, pl.ds(pl.multiple_of(kb * blk, blk), blk), :]
        if own:
            key_i = lax.broadcasted_iota(jnp.int32, (blk, blk), 0)
            qry_i = lax.broadcasted_iota(jnp.int32, (blk, blk), 1)
            causal = key_i <= qry_i

        def logits(h):
            p = h // PAIR
            return _dot(k[:, p * LANES:(p + 1) * LANES], qs_ref[h])

        ahead = [logits(h) for h in range(LOGITS_AHEAD)]
        for h in range(N_HEADS):
            rows = slice(h * HEAD_DIM, (h + 1) * HEAD_DIM)
            s = ahead.pop(0)
            if h + LOGITS_AHEAD < N_HEADS:
                ahead.append(logits(h + LOGITS_AHEAD))
            vt = vt_ref[0, kb, rows, :]
            if own:
                s = jnp.where(causal, s, NEG)
                m_new = jnp.max(s, axis=0, keepdims=True)
                pr = jnp.exp2(s - m_new)
                l_ref[h] = jnp.sum(pr, axis=0, keepdims=True)
                acc_ref[rows, :] = _dot(vt, pr.astype(BF16))
            else:
                taken = sel_ref[h, pl.ds(kb, 1), :] > 0.0
                m_i = m_ref[h]
                m_new = jnp.where(taken, jnp.maximum(m_i, jnp.max(s, axis=0, keepdims=True)), m_i)
                alpha = jnp.exp2(m_i - m_new)
                pr = jnp.exp2(s - jnp.where(taken, m_new, -NEG))
                l_ref[h] = alpha * l_ref[h] + jnp.sum(pr, axis=0, keepdims=True)
                acc_ref[rows, :] = alpha * acc_ref[rows, :] + _dot(vt, pr.astype(BF16))
            m_ref[h] = m_new

    step(True, qb)

    def past(kb, carry):
        step(False, kb)
        return carry

    lax.fori_loop(0, qb, past, 0)
    outs = [acc_ref[h * HEAD_DIM:(h + 1) * HEAD_DIM, :] / l_ref[h] for h in range(N_HEADS)]
    o_ref[0] = jnp.concatenate(outs, axis=0).T


def _prompt_attn(qt, kbf, vt, km):
    b, nb, _, blk = qt.shape
    l = nb * blk
    return pl.pallas_call(
        _prompt_attn_kernel,
        grid=(b, nb),
        in_specs=[pl.BlockSpec((1, 1, WIDTH, blk), lambda i, j: (i, j, 0, 0)),
                  pl.BlockSpec((1, l, WIDTH), lambda i, j: (i, 0, 0)),
                  pl.BlockSpec((1, nb, WIDTH, blk), lambda i, j: (i, 0, 0, 0)),
                  pl.BlockSpec((1, nb, WIDTH), lambda i, j: (i, 0, 0))],
        out_specs=pl.BlockSpec((1, blk, WIDTH), lambda i, j: (i, j, 0)),
        out_shape=jax.ShapeDtypeStruct((b, l, WIDTH), F32),
        scratch_shapes=[pltpu.VMEM((N_HEADS, LANES, blk), BF16),
                        pltpu.VMEM((N_HEADS, nb, blk), F32),
                        pltpu.VMEM((N_HEADS, 1, blk), F32),
                        pltpu.VMEM((N_HEADS, 1, blk), F32),
                        pltpu.VMEM((WIDTH, blk), F32)],
        compiler_params=pltpu.CompilerParams(dimension_semantics=("parallel", "parallel"),
                                             vmem_limit_bytes=VMEM_LIMIT),
        name="prompt_attn",
    )(qt, kbf, vt, km)


def _out_mlp(x, a_n, b_out, sh_f, sc_f, g_m, g_f, onb, gffn, wout_ref, w1_ref, w2_ref):
    b_n = _rms_rows(b_out, onb).astype(BF16)
    mixed = _dot(a_n, wout_ref[0:WIDTH, :]) + _dot(b_n, wout_ref[WIDTH:2 * WIDTH, :])
    x1 = x + g_m * mixed
    h2 = (_rms_rows(x1, gffn) * (1.0 + sc_f) + sh_f).astype(BF16)
    acc = jnp.zeros_like(x1)
    for c in range(D_FF // D_MODEL):
        t = jnp.maximum(_dot(h2, w1_ref[:, c * D_MODEL:(c + 1) * D_MODEL]), 0.0)
        acc = acc + _dot((t * t).astype(BF16), w2_ref[c * D_MODEL:(c + 1) * D_MODEL, :])
    return x1 + g_f * acc


DIM_GROUPS = HEAD_DIM // SUBLANES


def _block_gate_partials(q_row, page_refs):
    qt = _row_as_columns(q_row)
    n_blocks = len(page_refs) // PAGES_PER_BLOCK
    heads = []
    for h in range(N_HEADS):
        tot = [None] * n_blocks
        for g in range(DIM_GROUPS):
            rows = slice(g * SUBLANES, (g + 1) * SUBLANES)
            qv = qt[h * HEAD_DIM + g * SUBLANES:h * HEAD_DIM + (g + 1) * SUBLANES, :]
            for t in range(n_blocks):
                for u in range(PAGES_PER_BLOCK):
                    term = page_refs[PAGES_PER_BLOCK * t + u][0, 0, h, rows, :] * qv
                    tot[t] = term if tot[t] is None else tot[t] + term
        heads.append(jnp.stack(tot, axis=0))
    return jnp.stack(heads, axis=1)


def _prompt_out_kernel(steps_per_seq, pt_ref, x_ref, an_ref, b_ref, mod_ref, onb_ref, gffn_ref,
                       wout_ref, w1_ref, w2_ref, qs_ref, *refs):
    page_refs, (y_ref, g_ref) = refs[:-2], refs[-2:]
    step = pl.program_id(0) * pl.num_programs(1) + pl.program_id(1)
    y_ref[0] = _out_mlp(x_ref[0], an_ref[0], b_ref[0],
                        mod_ref[0, 3:4, :], mod_ref[0, 4:5, :], mod_ref[0, 2:3, :], mod_ref[0, 5:6, :],
                        onb_ref[...], gffn_ref[...], wout_ref, w1_ref, w2_ref)
    g_ref[0] = _block_gate_partials(qs_ref[pl.ds(step // steps_per_seq, 1), :], page_refs)


def _prompt_out(l_idx, x, a_n, b_out, mod, on_b, g_ffn, w_out, w1, w2, q_s, page_table, cache_kt, tm=256):
    b, l, _ = x.shape
    nt = l // tm
    ns, n_pages = page_table.shape
    steps_per_seq = (b * nt) // ns
    pages_per_step = n_pages // steps_per_seq
    blocks_per_step = pages_per_step // PAGES_PER_BLOCK
    assert steps_per_seq * ns == b * nt and pages_per_step * steps_per_seq == n_pages
    assert blocks_per_step * PAGES_PER_BLOCK == pages_per_step

    def page_map(t):
        return lambda i, j, pt: (l_idx, pt[(i * nt + j) * pages_per_step + t], 0, 0, 0)

    const2 = lambda i, j, pt: (0, 0)
    rows = lambda i, j, pt: (i, j, 0)
    once = pl.Buffered(1)
    gate_block = (blocks_per_step, N_HEADS, SUBLANES, LANES)
    grid_spec = pltpu.PrefetchScalarGridSpec(
        num_scalar_prefetch=1,
        grid=(b, nt),
        in_specs=[pl.BlockSpec((1, tm, D_MODEL), rows),
                  pl.BlockSpec((1, tm, WIDTH), rows),
                  pl.BlockSpec((1, tm, WIDTH), rows),
                  pl.BlockSpec((1, 6, D_MODEL), lambda i, j, pt: (i, 0, 0)),
                  pl.BlockSpec((1, WIDTH), const2),
                  pl.BlockSpec((1, D_MODEL), const2),
                  pl.BlockSpec((2 * WIDTH, D_MODEL), const2, pipeline_mode=once),
                  pl.BlockSpec((D_MODEL, D_FF), const2, pipeline_mode=once),
                  pl.BlockSpec((D_FF, D_MODEL), const2, pipeline_mode=once),
                  pl.BlockSpec((ns, WIDTH), const2)]
                 + [pl.BlockSpec((1, 1, N_HEADS, HEAD_DIM, PAGE_SIZE), page_map(t)) for t in range(pages_per_step)],
        out_specs=[pl.BlockSpec((1, tm, D_MODEL), rows),
                   pl.BlockSpec((1,) + gate_block, lambda i, j, pt: (i * nt + j, 0, 0, 0, 0))])
    y, g = pl.pallas_call(
        functools.partial(_prompt_out_kernel, steps_per_seq),
        grid_spec=grid_spec,
        out_shape=[jax.ShapeDtypeStruct(x.shape, F32),
                   jax.ShapeDtypeStruct((b * nt,) + gate_block, F32)],
        compiler_params=pltpu.CompilerParams(dimension_semantics=("parallel", "parallel"),
                                             vmem_limit_bytes=VMEM_LIMIT),
        name="prompt_out",
    )(page_table.reshape(-1), x, a_n, b_out, mod, on_b, g_ffn, w_out, w1, w2, q_s,
      *([cache_kt] * pages_per_step))
    return y, g.reshape(ns, n_pages // PAGES_PER_BLOCK, N_HEADS, SUBLANES, LANES)


def _sample_out_kernel(x_ref, an_ref, b_ref, mod_ref, onb_ref, gffn_ref, wout_ref, w1_ref, w2_ref, y_ref):
    y_ref[...] = _out_mlp(x_ref[...], an_ref[...], b_ref[...],
                          mod_ref[3], mod_ref[4], mod_ref[2], mod_ref[5],
                          onb_ref[...], gffn_ref[...], wout_ref, w1_ref, w2_ref)


def _sample_out(x, a_n, b_out, mod_t, on_b, g_ffn, w_out, w1, w2):
    return pl.pallas_call(
        _sample_out_kernel,
        out_shape=jax.ShapeDtypeStruct(x.shape, F32),
        compiler_params=pltpu.CompilerParams(vmem_limit_bytes=VMEM_LIMIT),
        name="sample_out",
    )(x, a_n, b_out, mod_t, on_b, g_ffn, w_out, w1, w2)


def _sample_in_kernel(x_ref, mod_ref, gmix_ref, win_ref, bd_ref, qg_ref, kg_ref, vg_ref,
                      ws0_ref, bs0_ref, ona_ref,
                      q_ref, k_ref, v_ref, cv_ref, an_ref):
    z = _in_proj(x_ref[...], mod_ref[0], mod_ref[1], gmix_ref[...], win_ref[...])
    bd = bd_ref[...]
    ua, va = z[:, 0:WIDTH], z[:, WIDTH:2 * WIDTH]
    q, k, v = z[:, 2 * WIDTH:3 * WIDTH], z[:, 3 * WIDTH:4 * WIDTH], z[:, 4 * WIDTH:5 * WIDTH]
    q_ref[...] = _head_rms(q, bd, qg_ref[...])
    k_ref[...] = _head_rms(k, bd, kg_ref[...])
    v_ref[...] = v
    vn = _head_rms(jax.nn.gelu(va), bd, vg_ref[...])
    cv_ref[...] = vn
    mixed = ws0_ref[...] * vn + bs0_ref[...]
    an_ref[...] = _rms_rows(jax.nn.gelu(ua) * mixed, ona_ref[...]).astype(BF16)


def _sample_in(x, mod_t, g_mix, w_in, bd, qg, kg, vg, ws0, bs0, on_a):
    n = x.shape[0]
    row = jax.ShapeDtypeStruct((n, WIDTH), F32)
    return pl.pallas_call(
        _sample_in_kernel,
        out_shape=[row, row, row, row, jax.ShapeDtypeStruct((n, WIDTH), BF16)],
        compiler_params=pltpu.CompilerParams(vmem_limit_bytes=VMEM_LIMIT),
        name="sample_in",
    )(x, mod_t, g_mix, w_in, bd, qg, kg, vg, ws0, bs0, on_a)


def _row_as_columns(row):
    return jnp.broadcast_to(row, (LANES, WIDTH)).T


def _sample_top3_kernel(g_ref, sel_ref):
    nb = g_ref.shape[1]
    part = jnp.sum(g_ref[0], axis=3, keepdims=True)
    gate = jnp.sum(part, axis=2, keepdims=True) * (1.0 / MOBA_BLOCK)
    bidf = lax.broadcasted_iota(jnp.int32, gate.shape, 0).astype(F32)
    picks = []
    for _ in range(MOBA_TOPK):
        mx = jnp.max(gate, axis=0, keepdims=True)
        idx = jnp.min(jnp.where(gate == mx, bidf, float(nb)), axis=0, keepdims=True)
        picks.append(idx)
        gate = jnp.where(bidf == idx, -jnp.inf, gate)
    sel = jnp.concatenate(picks, axis=0)
    sel_ref[0] = jnp.broadcast_to(sel, sel_ref.shape[1:]).astype(jnp.int32)


def _sample_top3(g):
    n = g.shape[0]
    sel_block = (MOBA_TOPK, N_HEADS, SUBLANES, LANES)
    sel = pl.pallas_call(
        _sample_top3_kernel,
        grid=(n,),
        in_specs=[pl.BlockSpec((1,) + g.shape[1:], lambda s: (s, 0, 0, 0, 0))],
        out_specs=pl.BlockSpec((1,) + sel_block, lambda s: (s, 0, 0, 0, 0)),
        out_shape=jax.ShapeDtypeStruct((n,) + sel_block, jnp.int32),
        compiler_params=pltpu.CompilerParams(dimension_semantics=("parallel",),
                                             vmem_limit_bytes=VMEM_LIMIT),
        name="sample_top3",
    )(g)
    return sel[:, :, :, 0, 0]


SLABS = MOBA_TOPK * PAGES_PER_BLOCK
N_SLABS = N_HEADS * SLABS


def _sample_attn_kernel(l, ids_ref, q_ref, kn_ref, vn_ref, ck_ref, cv_ref, o_ref, kbuf, vbuf, sem):
    s = pl.program_id(0)
    n = pl.num_programs(0)
    slot = lax.rem(s, 2)

    def copies(seq, slot_):
        out = []
        for i in range(N_SLABS):
            page = ids_ref[seq * N_SLABS + i]
            h = i // SLABS
            out.append(pltpu.make_async_copy(ck_ref.at[l, page, h], kbuf.at[slot_, i], sem.at[0, slot_]))
            out.append(pltpu.make_async_copy(cv_ref.at[l, page, h], vbuf.at[slot_, i], sem.at[1, slot_]))
        return out

    @pl.when(s == 0)
    def _():
        for c in copies(0, 0):
            c.start()

    @pl.when(s + 1 < n)
    def _():
        for c in copies(s + 1, 1 - slot):
            c.start()

    for c in copies(s, slot):
        c.wait()

    qt = _row_as_columns(q_ref[pl.ds(s, 1), :] * SCALE)
    knt = _row_as_columns(kn_ref[pl.ds(s, 1), :])
    lane_head = lax.broadcasted_iota(jnp.int32, (1, WIDTH), 1) // HEAD_DIM
    pself_row = jnp.zeros((1, WIDTH), F32)
    l_row = jnp.zeros((1, WIDTH), F32)
    accs = []
    for h in range(N_HEADS):
        rows = slice(h * HEAD_DIM, (h + 1) * HEAD_DIM)
        qh = qt[rows]
        s_self = jnp.sum(qh * knt[rows], axis=0, keepdims=True)
        logits = [jnp.sum(kbuf[slot, h * SLABS + t] * qh, axis=0, keepdims=True) for t in range(SLABS)]
        mx = logits[0]
        for s_t in logits[1:]:
            mx = jnp.maximum(mx, s_t)
        m = jnp.maximum(s_self, jnp.max(mx, axis=1, keepdims=True))
        p_self = jnp.exp(s_self - m)
        l_h = p_self
        acc = jnp.zeros((HEAD_DIM, LANES), F32)
        for t, s_t in enumerate(logits):
            p = jnp.exp(s_t - m)
            l_h = l_h + jnp.sum(p, axis=1, keepdims=True)
            acc = acc + vbuf[slot, h * SLABS + t] * p
        accs.append(acc)
        pself_row = jnp.where(lane_head == h, p_self[:, 0:1], pself_row)
        l_row = jnp.where(lane_head == h, l_h[:, 0:1], l_row)
    pv = jnp.sum(jnp.concatenate(accs, axis=0).T, axis=0, keepdims=True)
    o_ref[pl.ds(s, 1), :] = (pv + pself_row * vn_ref[pl.ds(s, 1), :]) / l_row


def _sample_attn(l, page_ids, q, k_new, v_new, cache_kt, cache_vt):
    n = q.shape[0]
    full = pl.BlockSpec((n, WIDTH), lambda s, ids: (0, 0))
    grid_spec = pltpu.PrefetchScalarGridSpec(
        num_scalar_prefetch=1,
        grid=(n,),
        in_specs=[full, full, full, pl.BlockSpec(memory_space=pl.ANY), pl.BlockSpec(memory_space=pl.ANY)],
        out_specs=full,
        scratch_shapes=[pltpu.VMEM((2, N_SLABS, HEAD_DIM, PAGE_SIZE), F32),
                        pltpu.VMEM((2, N_SLABS, HEAD_DIM, PAGE_SIZE), F32),
                        pltpu.SemaphoreType.DMA((2, 2))])
    return pl.pallas_call(
        functools.partial(_sample_attn_kernel, l),
        grid_spec=grid_spec,
        out_shape=jax.ShapeDtypeStruct((n, WIDTH), F32),
        compiler_params=pltpu.CompilerParams(dimension_semantics=("arbitrary",),
                                             vmem_limit_bytes=VMEM_LIMIT),
        name="sample_attn",
    )(page_ids, q, k_new, v_new, cache_kt, cache_vt)


def _head_tile(g):
    return jnp.tile(g, N_HEADS).reshape(1, WIDTH)


def _layer(l, x_prompt, x_sample, cache_kt, cache_vt, page_table, c_prompt, c_sample,
           norm_mix_g, norm_ffn_g, w_ada, b_ada, w_in, qn_g, kn_g, vn_g, w_s, b_s,
           on_a, on_b, w_out, w1, w2):
    nb_, seq = x_prompt.shape[0], x_prompt.shape[1]
    ns = x_sample.shape[0]

    w_in_b, w_out_b = w_in[l].astype(BF16), w_out[l].astype(BF16)
    w1_b, w2_b = w1[l].astype(BF16), w2[l].astype(BF16)
    g_mix, g_ffn = norm_mix_g[l].reshape(1, D_MODEL), norm_ffn_g[l].reshape(1, D_MODEL)
    qg, kg, vg = _head_tile(qn_g[l]), _head_tile(kn_g[l]), _head_tile(vn_g[l])
    ona, onb = on_a[l].reshape(1, WIDTH), on_b[l].reshape(1, WIDTH)
    head_of = jnp.arange(2 * LANES) // HEAD_DIM
    bd = (head_of[:, None] == head_of[None, :]).astype(BF16)
    bias = jnp.repeat(b_s[l].T, HEAD_DIM, axis=1)
    ws0 = jnp.repeat(w_s[l][:, 0, 0], HEAD_DIM).reshape(1, WIDTH)
    bs0 = jnp.repeat(b_s[l][:, 0], HEAD_DIM).reshape(1, WIDTH)

    mod = _ada(jnp.concatenate([c_prompt, c_sample], axis=0), w_ada[l], b_ada[l])
    mod_p = mod[:nb_].reshape(nb_, 6, D_MODEL)
    mod_s = mod[nb_:].reshape(ns, 6, D_MODEL).transpose(1, 0, 2)

    xs = x_sample.reshape(ns, D_MODEL)
    q_s, k_s, v_s, cv_s, an_s = _sample_in(xs, mod_s, g_mix, w_in_b, bd, qg, kg, vg, ws0, bs0, ona)

    k_p, v_p, kbf, km, qt, vt, an_p = _prompt_in(x_prompt, mod_p, g_mix, w_in_b, bd, qg, kg, vg,
                                                 w_s[l], bias, ona)
    b_p = _prompt_attn(qt, kbf, vt, km)
    y_p, gate_parts = _prompt_out(l, x_prompt, an_p, b_p, mod_p, onb, g_ffn, w_out_b, w1_b, w2_b,
                                  q_s, page_table, cache_kt)

    sel = _sample_top3(gate_parts)
    logical = sel.transpose(0, 2, 1)[..., None] * PAGES_PER_BLOCK + jnp.arange(PAGES_PER_BLOCK)
    page_ids = jnp.take_along_axis(page_table, logical.reshape(ns, -1), axis=1)
    b_s_out = _sample_attn(l, page_ids.reshape(-1), q_s, k_s, v_s, cache_kt, cache_vt)
    y_s = _sample_out(xs, an_s, b_s_out, mod_s, onb, g_ffn, w_out_b, w1_b, w2_b)

    hshape = (N_HEADS, HEAD_DIM)
    return (y_p, y_s.reshape(x_sample.shape),
            k_p.reshape(nb_, seq, *hshape), v_p.reshape(nb_, seq, *hshape),
            k_s.reshape(ns, 1, *hshape), v_s.reshape(ns, 1, *hshape), cv_s.reshape(ns, 1, *hshape))


def kernel(x_prompt, x_sample, cache_k, cache_v, page_table, c_prompt, c_sample, norm_mix_g, norm_ffn_g,
           w_ada, b_ada, w_in, qn_g, kn_g, vn_g, w_s, b_s, on_a, on_b, w_out, w1, w2):
    depth = w_in.shape[0]
    cache_kt = jnp.transpose(cache_k, (0, 1, 3, 4, 2))
    cache_vt = jnp.transpose(cache_v, (0, 1, 3, 4, 2))
    x_p, x_s = x_prompt, x_sample
    kp, vp, ks, vs, cv = [], [], [], [], []
    for l in range(depth):
        x_p, x_s, k_p, v_p, k_s, v_s, cv_s = _layer(
            l, x_p, x_s, cache_kt, cache_vt, page_table, c_prompt, c_sample, norm_mix_g, norm_ffn_g,
            w_ada, b_ada, w_in, qn_g, kn_g, vn_g, w_s, b_s, on_a, on_b, w_out, w1, w2)
        kp.append(k_p); vp.append(v_p); ks.append(k_s); vs.append(v_s); cv.append(cv_s)
    return (x_p, x_s, jnp.stack(kp), jnp.stack(vp), jnp.stack(ks), jnp.stack(vs), jnp.stack(cv))
```

```python
import functools

import jax
import jax.numpy as jnp
from jax import lax
from jax.experimental import pallas as pl
from jax.experimental.pallas import tpu as pltpu

D_MODEL = 1024
HEAD_DIM = 64
N_HEADS = 8
WIDTH = N_HEADS * HEAD_DIM
IN_COLS = 5 * WIDTH
CHUNK = 128
MOBA_BLOCK = 256
MOBA_TOPK = 3
PAGE_SIZE = 128
PAGES_PER_BLOCK = MOBA_BLOCK // PAGE_SIZE
D_FF = 4 * D_MODEL
EPS = 1e-6
NEG = -1e30
SCALE = HEAD_DIM ** -0.5
LOG2E = 1.4426950408889634

LANES = 128
SUBLANES = 8
PAIR = LANES // HEAD_DIM
N_PAIRS = N_HEADS // PAIR
LOGITS_AHEAD = 3
VMEM_LIMIT = 56 * 1024 * 1024

F32 = jnp.float32
BF16 = jnp.bfloat16


def _dot(a, b):
    return jnp.dot(a, b, preferred_element_type=F32)


def _rms_rows(x, g):
    return x * lax.rsqrt(jnp.mean(x * x, axis=-1, keepdims=True) + EPS) * g


def _head_rms(t, bd, g):
    t2 = (t * t).astype(BF16)
    half = bd.shape[0]
    ss = jnp.concatenate([_dot(t2[:, :half], bd), _dot(t2[:, half:], bd)], axis=-1)
    return t * lax.rsqrt(ss * (1.0 / HEAD_DIM) + EPS) * g


def _in_proj(x, sh, sc, g_mix, w_in):
    h = _rms_rows(x, g_mix) * (1.0 + sc) + sh
    return _dot(h.astype(BF16), w_in)


def _ada_kernel(c_ref, w_ref, b_ref, o_ref):
    s = jax.nn.silu(c_ref[...])
    o_ref[...] = jnp.dot(s, w_ref[...], preferred_element_type=F32,
                         precision=lax.Precision.HIGHEST) + b_ref[...]


def _ada(c_all, w_ada, b_ada):
    n, tn = c_all.shape[0], 512
    cols = w_ada.shape[1]
    return pl.pallas_call(
        _ada_kernel,
        grid=(cols // tn,),
        in_specs=[pl.BlockSpec((n, D_MODEL), lambda j: (0, 0)),
                  pl.BlockSpec((D_MODEL, tn), lambda j: (0, j)),
                  pl.BlockSpec((1, tn), lambda j: (0, j))],
        out_specs=pl.BlockSpec((n, tn), lambda j: (0, j)),
        out_shape=jax.ShapeDtypeStruct((n, cols), F32),
        compiler_params=pltpu.CompilerParams(dimension_semantics=("parallel",),
                                             vmem_limit_bytes=VMEM_LIMIT),
        name="ada_mod",
    )(c_all, w_ada, b_ada.reshape(1, cols))


def _prompt_in_kernel(x_ref, mod_ref, gmix_ref, win_ref, bd_ref, qg_ref, kg_ref, vg_ref,
                      ws_ref, bias_ref, ona_ref,
                      kt_ref, vtf_ref, kbf_ref, km_ref, qt_ref, vt_ref, an_ref):
    tm = x_ref.shape[1]
    j = pl.program_id(1)
    z = _in_proj(x_ref[0], mod_ref[0, 0:1, :], mod_ref[0, 1:2, :], gmix_ref[...], win_ref[...])
    bd = bd_ref[...]
    ua, va = z[:, 0:WIDTH], z[:, WIDTH:2 * WIDTH]
    q, k, v = z[:, 2 * WIDTH:3 * WIDTH], z[:, 3 * WIDTH:4 * WIDTH], z[:, 4 * WIDTH:5 * WIDTH]

    k_n = _head_rms(k, bd, kg_ref[...])
    kt_ref[0] = k_n.T
    kbf_ref[0] = k_n.astype(BF16)
    km_ref[0, pl.ds(j, 1), :] = jnp.mean(k_n, axis=0, keepdims=True)
    v_t = v.T
    vtf_ref[0] = v_t
    qt_ref[0, 0] = _head_rms(q, bd, qg_ref[...]).T
    vt_ref[0, 0] = v_t.astype(BF16)

    vn = _head_rms(jax.nn.gelu(va), bd, vg_ref[...]).astype(BF16)
    row = lax.broadcasted_iota(jnp.int32, (CHUNK, CHUNK), 0)
    col = lax.broadcasted_iota(jnp.int32, (CHUNK, CHUNK), 1)
    ws = [jnp.where(row >= col, ws_ref[h], 0.0).astype(BF16) for h in range(N_HEADS)]
    lane = lax.broadcasted_iota(jnp.int32, (CHUNK, LANES), 1)
    chunks = []
    for c in range(tm // CHUNK):
        pieces = []
        for p in range(N_PAIRS):
            vp = vn[c * CHUNK:(c + 1) * CHUNK, p * LANES:(p + 1) * LANES]
            pieces.append(jnp.where(lane < HEAD_DIM, _dot(ws[PAIR * p], vp), _dot(ws[PAIR * p + 1], vp)))
        chunks.append(jnp.concatenate(pieces, axis=-1) + bias_ref[...])
    mixed = jnp.concatenate(chunks, axis=0)
    an_ref[0] = _rms_rows(jax.nn.gelu(ua) * mixed, ona_ref[...]).astype(BF16)


def _prompt_in(x, mod, g_mix, w_in, bd, qg, kg, vg, w_s, bias, on_a):
    b, l, _ = x.shape
    tm = MOBA_BLOCK
    nt = l // tm
    const2 = lambda i, j: (0, 0)
    rows = pl.BlockSpec((1, tm, WIDTH), lambda i, j: (i, j, 0))
    cols = pl.BlockSpec((1, 1, WIDTH, tm), lambda i, j: (i, j, 0, 0))
    full_cols = pl.BlockSpec((1, WIDTH, tm), lambda i, j: (i, 0, j))
    return pl.pallas_call(
        _prompt_in_kernel,
        grid=(b, nt),
        in_specs=[pl.BlockSpec((1, tm, D_MODEL), lambda i, j: (i, j, 0)),
                  pl.BlockSpec((1, 6, D_MODEL), lambda i, j: (i, 0, 0)),
                  pl.BlockSpec((1, D_MODEL), const2),
                  pl.BlockSpec((D_MODEL, IN_COLS), const2),
                  pl.BlockSpec(bd.shape, const2),
                  pl.BlockSpec((1, WIDTH), const2),
                  pl.BlockSpec((1, WIDTH), const2),
                  pl.BlockSpec((1, WIDTH), const2),
                  pl.BlockSpec((N_HEADS, CHUNK, CHUNK), lambda i, j: (0, 0, 0)),
                  pl.BlockSpec((CHUNK, WIDTH), const2),
                  pl.BlockSpec((1, WIDTH), const2)],
        out_specs=[full_cols, full_cols, rows,
                   pl.BlockSpec((1, nt, WIDTH), lambda i, j: (i, 0, 0)),
                   cols, cols, rows],
        out_shape=[jax.ShapeDtypeStruct((b, WIDTH, l), F32),
                   jax.ShapeDtypeStruct((b, WIDTH, l), F32),
                   jax.ShapeDtypeStruct((b, l, WIDTH), BF16),
                   jax.ShapeDtypeStruct((b, nt, WIDTH), F32),
                   jax.ShapeDtypeStruct((b, nt, WIDTH, tm), F32),
                   jax.ShapeDtypeStruct((b, nt, WIDTH, tm), BF16),
                   jax.ShapeDtypeStruct((b, l, WIDTH), BF16)],
        compiler_params=pltpu.CompilerParams(dimension_semantics=("parallel", "arbitrary"),
                                             vmem_limit_bytes=VMEM_LIMIT),
        name="prompt_in",
    )(x, mod, g_mix, w_in, bd, qg, kg, vg, w_s, bias, on_a)


def _prompt_attn_kernel(qt_ref, k_ref, vt_ref, km_ref, o_ref, qs_ref, sel_ref, m_ref, l_ref, acc_ref):
    qb = pl.program_id(1)
    nb = km_ref.shape[1]
    blk = MOBA_BLOCK

    qt = qt_ref[0, 0]
    km = km_ref[0]
    lane = lax.broadcasted_iota(jnp.int32, (nb, LANES), 1)
    drow = lax.broadcasted_iota(jnp.int32, (LANES, blk), 0)
    bid = lax.broadcasted_iota(jnp.int32, (nb, blk), 0)
    for h in range(N_HEADS):
        p, hh = divmod(h, PAIR)
        lo = hh * HEAD_DIM
        qp = qt[p * LANES:(p + 1) * LANES]
        qs_ref[h] = jnp.where((drow >= lo) & (drow < lo + HEAD_DIM), qp * (SCALE * LOG2E), 0.0).astype(BF16)
        kmm = jnp.where((lane >= lo) & (lane < lo + HEAD_DIM), km[:, p * LANES:(p + 1) * LANES], 0.0)
        gate = jnp.dot(kmm, qp, preferred_element_type=F32, precision=lax.Precision.HIGHEST)
        cnt = jnp.zeros((nb, blk), jnp.int32)
        for m in range(nb):
            gm = gate[m:m + 1, :]
            beats = (gm > gate) | ((gm == gate) & (bid > m))
            cnt = cnt + jnp.where(beats & (qb > m), 1, 0)
        sel_ref[h] = jnp.where((cnt < MOBA_TOPK) & (bid < qb), 1.0, 0.0)

    def step(own, kb):
        k = k_ref[0, pl.ds(pl.multiple_of(kb * blk, blk), blk), :]
        if own:
            key_i = lax.broadcasted_iota(jnp.int32, (blk, blk), 0)
            qry_i = lax.broadcasted_iota(jnp.int32, (blk, blk), 1)
            causal = key_i <= qry_i

        def logits(h):
            p = h // PAIR
            return _dot(k[:, p * LANES:(p + 1) * LANES], qs_ref[h])

        ahead = [logits(h) for h in range(LOGITS_AHEAD)]
        for h in range(N_HEADS):
            rows = slice(h * HEAD_DIM, (h + 1) * HEAD_DIM)
            s = ahead.pop(0)
            if h + LOGITS_AHEAD < N_HEADS:
                ahead.append(logits(h + LOGITS_AHEAD))
            vt = vt_ref[0, kb, rows, :]
            if own:
                s = jnp.where(causal, s, NEG)
                m_new = jnp.max(s, axis=0, keepdims=True)
                pr = jnp.exp2(s - m_new)
                l_ref[h] = jnp.sum(pr, axis=0, keepdims=True)
                acc_ref[rows, :] = _dot(vt, pr.astype(BF16))
            else:
                taken = sel_ref[h, pl.ds(kb, 1), :] > 0.0
                m_i = m_ref[h]
                m_new = jnp.where(taken, jnp.maximum(m_i, jnp.max(s, axis=0, keepdims=True)), m_i)
                alpha = jnp.exp2(m_i - m_new)
                pr = jnp.exp2(s - jnp.where(taken, m_new, -NEG))
                l_ref[h] = alpha * l_ref[h] + jnp.sum(pr, axis=0, keepdims=True)
                acc_ref[rows, :] = alpha * acc_ref[rows, :] + _dot(vt, pr.astype(BF16))
            m_ref[h] = m_new

    step(True, qb)

    def past(kb, carry):
        step(False, kb)
        return carry

    lax.fori_loop(0, qb, past, 0)
    outs = [acc_ref[h * HEAD_DIM:(h + 1) * HEAD_DIM, :] / l_ref[h] for h in range(N_HEADS)]
    o_ref[0] = jnp.concatenate(outs, axis=0).T


def _prompt_attn(qt, kbf, vt, km):
    b, nb, _, blk = qt.shape
    l = nb * blk
    return pl.pallas_call(
        _prompt_attn_kernel,
        grid=(b, nb),
        in_specs=[pl.BlockSpec((1, 1, WIDTH, blk), lambda i, j: (i, j, 0, 0)),
                  pl.BlockSpec((1, l, WIDTH), lambda i, j: (i, 0, 0)),
                  pl.BlockSpec((1, nb, WIDTH, blk), lambda i, j: (i, 0, 0, 0)),
                  pl.BlockSpec((1, nb, WIDTH), lambda i, j: (i, 0, 0))],
        out_specs=pl.BlockSpec((1, blk, WIDTH), lambda i, j: (i, j, 0)),
        out_shape=jax.ShapeDtypeStruct((b, l, WIDTH), F32),
        scratch_shapes=[pltpu.VMEM((N_HEADS, LANES, blk), BF16),
                        pltpu.VMEM((N_HEADS, nb, blk), F32),
                        pltpu.VMEM((N_HEADS, 1, blk), F32),
                        pltpu.VMEM((N_HEADS, 1, blk), F32),
                        pltpu.VMEM((WIDTH, blk), F32)],
        compiler_params=pltpu.CompilerParams(dimension_semantics=("parallel", "parallel"),
                                             vmem_limit_bytes=VMEM_LIMIT),
        name="prompt_attn",
    )(qt, kbf, vt, km)


def _out_mlp(x, a_n, b_out, sh_f, sc_f, g_m, g_f, onb, gffn, wout_ref, w1_ref, w2_ref):
    b_n = _rms_rows(b_out, onb).astype(BF16)
    mixed = _dot(a_n, wout_ref[0:WIDTH, :]) + _dot(b_n, wout_ref[WIDTH:2 * WIDTH, :])
    x1 = x + g_m * mixed
    h2 = (_rms_rows(x1, gffn) * (1.0 + sc_f) + sh_f).astype(BF16)
    acc = jnp.zeros_like(x1)
    for c in range(D_FF // D_MODEL):
        t = jnp.maximum(_dot(h2, w1_ref[:, c * D_MODEL:(c + 1) * D_MODEL]), 0.0)
        acc = acc + _dot((t * t).astype(BF16), w2_ref[c * D_MODEL:(c + 1) * D_MODEL, :])
    return x1 + g_f * acc


DIM_GROUPS = HEAD_DIM // SUBLANES


def _block_gate_partials(q_row, page_refs):
    qt = _row_as_columns(q_row)
    n_blocks = len(page_refs) // PAGES_PER_BLOCK
    sub = lax.broadcasted_iota(jnp.int32, (N_HEADS, LANES), 0)
    blocks = [jnp.zeros((N_HEADS, LANES), F32)] * n_blocks
    for h in range(N_HEADS):
        tot = [None] * n_blocks
        for g in range(DIM_GROUPS):
            rows = slice(g * SUBLANES, (g + 1) * SUBLANES)
            qv = qt[h * HEAD_DIM + g * SUBLANES:h * HEAD_DIM + (g + 1) * SUBLANES, :]
            for t in range(n_blocks):
                for u in range(PAGES_PER_BLOCK):
                    term = page_refs[PAGES_PER_BLOCK * t + u][0, 0, h, rows, :] * qv
                    tot[t] = term if tot[t] is None else tot[t] + term
        for t in range(n_blocks):
            blocks[t] = jnp.where(sub == h, jnp.sum(tot[t], axis=0, keepdims=True), blocks[t])
    return jnp.stack(blocks, axis=0)


def _prompt_out_kernel(steps_per_seq, pt_ref, x_ref, an_ref, b_ref, mod_ref, onb_ref, gffn_ref,
                       wout_ref, w1_ref, w2_ref, qs_ref, *refs):
    page_refs, (y_ref, g_ref) = refs[:-2], refs[-2:]
    step = pl.program_id(0) * pl.num_programs(1) + pl.program_id(1)
    y_ref[0] = _out_mlp(x_ref[0], an_ref[0], b_ref[0],
                        mod_ref[0, 3:4, :], mod_ref[0, 4:5, :], mod_ref[0, 2:3, :], mod_ref[0, 5:6, :],
                        onb_ref[...], gffn_ref[...], wout_ref, w1_ref, w2_ref)
    g_ref[0] = _block_gate_partials(qs_ref[pl.ds(step // steps_per_seq, 1), :], page_refs)


def _prompt_out(l_idx, x, a_n, b_out, mod, on_b, g_ffn, w_out, w1, w2, q_s, page_table, cache_kt, tm=256):
    b, l, _ = x.shape
    nt = l // tm
    ns, n_pages = page_table.shape
    steps_per_seq = (b * nt) // ns
    pages_per_step = n_pages // steps_per_seq
    blocks_per_step = pages_per_step // PAGES_PER_BLOCK
    assert steps_per_seq * ns == b * nt and pages_per_step * steps_per_seq == n_pages
    assert blocks_per_step * PAGES_PER_BLOCK == pages_per_step

    def page_map(t):
        return lambda i, j, pt: (l_idx, pt[(i * nt + j) * pages_per_step + t], 0, 0, 0)

    const2 = lambda i, j, pt: (0, 0)
    rows = lambda i, j, pt: (i, j, 0)
    once = pl.Buffered(1)
    gate_block = (blocks_per_step, N_HEADS, LANES)
    grid_spec = pltpu.PrefetchScalarGridSpec(
        num_scalar_prefetch=1,
        grid=(b, nt),
        in_specs=[pl.BlockSpec((1, tm, D_MODEL), rows),
                  pl.BlockSpec((1, tm, WIDTH), rows),
                  pl.BlockSpec((1, tm, WIDTH), rows),
                  pl.BlockSpec((1, 6, D_MODEL), lambda i, j, pt: (i, 0, 0)),
                  pl.BlockSpec((1, WIDTH), const2),
                  pl.BlockSpec((1, D_MODEL), const2),
                  pl.BlockSpec((2 * WIDTH, D_MODEL), const2, pipeline_mode=once),
                  pl.BlockSpec((D_MODEL, D_FF), const2, pipeline_mode=once),
                  pl.BlockSpec((D_FF, D_MODEL), const2, pipeline_mode=once),
                  pl.BlockSpec((ns, WIDTH), const2)]
                 + [pl.BlockSpec((1, 1, N_HEADS, HEAD_DIM, PAGE_SIZE), page_map(t)) for t in range(pages_per_step)],
        out_specs=[pl.BlockSpec((1, tm, D_MODEL), rows),
                   pl.BlockSpec((1,) + gate_block, lambda i, j, pt: (i * nt + j, 0, 0, 0))])
    y, g = pl.pallas_call(
        functools.partial(_prompt_out_kernel, steps_per_seq),
        grid_spec=grid_spec,
        out_shape=[jax.ShapeDtypeStruct(x.shape, F32),
                   jax.ShapeDtypeStruct((b * nt,) + gate_block, F32)],
        compiler_params=pltpu.CompilerParams(dimension_semantics=("parallel", "parallel"),
                                             vmem_limit_bytes=VMEM_LIMIT),
        name="prompt_out",
    )(page_table.reshape(-1), x, a_n, b_out, mod, on_b, g_ffn, w_out, w1, w2, q_s,
      *([cache_kt] * pages_per_step))
    return y, g.reshape(ns, n_pages // PAGES_PER_BLOCK, N_HEADS, LANES)


def _sample_out_kernel(x_ref, an_ref, b_ref, mod_ref, onb_ref, gffn_ref, wout_ref, w1_ref, w2_ref, y_ref):
    y_ref[...] = _out_mlp(x_ref[...], an_ref[...], b_ref[...],
                          mod_ref[3], mod_ref[4], mod_ref[2], mod_ref[5],
                          onb_ref[...], gffn_ref[...], wout_ref, w1_ref, w2_ref)


def _sample_out(x, a_n, b_out, mod_t, on_b, g_ffn, w_out, w1, w2):
    return pl.pallas_call(
        _sample_out_kernel,
        out_shape=jax.ShapeDtypeStruct(x.shape, F32),
        compiler_params=pltpu.CompilerParams(vmem_limit_bytes=VMEM_LIMIT),
        name="sample_out",
    )(x, a_n, b_out, mod_t, on_b, g_ffn, w_out, w1, w2)


def _sample_in_kernel(x_ref, mod_ref, gmix_ref, win_ref, bd_ref, qg_ref, kg_ref, vg_ref,
                      ws0_ref, bs0_ref, ona_ref,
                      q_ref, k_ref, v_ref, cv_ref, an_ref):
    z = _in_proj(x_ref[...], mod_ref[0], mod_ref[1], gmix_ref[...], win_ref[...])
    bd = bd_ref[...]
    ua, va = z[:, 0:WIDTH], z[:, WIDTH:2 * WIDTH]
    q, k, v = z[:, 2 * WIDTH:3 * WIDTH], z[:, 3 * WIDTH:4 * WIDTH], z[:, 4 * WIDTH:5 * WIDTH]
    q_ref[...] = _head_rms(q, bd, qg_ref[...])
    k_ref[...] = _head_rms(k, bd, kg_ref[...])
    v_ref[...] = v
    vn = _head_rms(jax.nn.gelu(va), bd, vg_ref[...])
    cv_ref[...] = vn
    mixed = ws0_ref[...] * vn + bs0_ref[...]
    an_ref[...] = _rms_rows(jax.nn.gelu(ua) * mixed, ona_ref[...]).astype(BF16)


def _sample_in(x, mod_t, g_mix, w_in, bd, qg, kg, vg, ws0, bs0, on_a):
    n = x.shape[0]
    row = jax.ShapeDtypeStruct((n, WIDTH), F32)
    return pl.pallas_call(
        _sample_in_kernel,
        out_shape=[row, row, row, row, jax.ShapeDtypeStruct((n, WIDTH), BF16)],
        compiler_params=pltpu.CompilerParams(vmem_limit_bytes=VMEM_LIMIT),
        name="sample_in",
    )(x, mod_t, g_mix, w_in, bd, qg, kg, vg, ws0, bs0, on_a)


def _row_as_columns(row):
    return jnp.broadcast_to(row, (LANES, WIDTH)).T


def _sample_top3_kernel(g_ref, sel_ref):
    nb = g_ref.shape[1]
    gate = jnp.sum(g_ref[0], axis=2, keepdims=True) * (1.0 / MOBA_BLOCK)
    bidf = lax.broadcasted_iota(jnp.int32, gate.shape, 0).astype(F32)
    picks = []
    for _ in range(MOBA_TOPK):
        mx = jnp.max(gate, axis=0, keepdims=True)
        idx = jnp.min(jnp.where(gate == mx, bidf, float(nb)), axis=0, keepdims=True)
        picks.append(idx)
        gate = jnp.where(bidf == idx, -jnp.inf, gate)
    sel = jnp.concatenate(picks, axis=0)
    sel_ref[0] = jnp.broadcast_to(sel, sel_ref.shape[1:]).astype(jnp.int32)


def _sample_top3(g):
    n = g.shape[0]
    sel_block = (MOBA_TOPK, N_HEADS, LANES)
    sel = pl.pallas_call(
        _sample_top3_kernel,
        grid=(n,),
        in_specs=[pl.BlockSpec((1,) + g.shape[1:], lambda s: (s, 0, 0, 0))],
        out_specs=pl.BlockSpec((1,) + sel_block, lambda s: (s, 0, 0, 0)),
        out_shape=jax.ShapeDtypeStruct((n,) + sel_block, jnp.int32),
        compiler_params=pltpu.CompilerParams(dimension_semantics=("parallel",),
                                             vmem_limit_bytes=VMEM_LIMIT),
        name="sample_top3",
    )(g)
    return sel[:, :, :, 0]


SLABS = MOBA_TOPK * PAGES_PER_BLOCK
N_SLABS = N_HEADS * SLABS


def _sample_attn_kernel(l, n_pages, sel_ref, pt_ref, q_ref, kn_ref, vn_ref, ck_ref, cv_ref, o_ref,
                        kbuf, vbuf, sem):
    s = pl.program_id(0)
    n = pl.num_programs(0)
    slot = lax.rem(s, 2)

    def slab_copies(page, h, i, slot_):
        return (pltpu.make_async_copy(ck_ref.at[l, page, h], kbuf.at[slot_, i], sem.at[0, slot_]),
                pltpu.make_async_copy(cv_ref.at[l, page, h], vbuf.at[slot_, i], sem.at[1, slot_]))

    def start_fetch(seq, slot_):
        for h in range(N_HEADS):
            for r in range(MOBA_TOPK):
                blk = sel_ref[(seq * MOBA_TOPK + r) * N_HEADS + h]
                for u in range(PAGES_PER_BLOCK):
                    page = pt_ref[seq * n_pages + blk * PAGES_PER_BLOCK + u]
                    for c in slab_copies(page, h, h * SLABS + r * PAGES_PER_BLOCK + u, slot_):
                        c.start()

    def wait_fetch(slot_):
        for i in range(N_SLABS):
            for c in slab_copies(0, 0, i, slot_):
                c.wait()

    @pl.when(s == 0)
    def _():
        start_fetch(0, 0)

    @pl.when(s + 1 < n)
    def _():
        start_fetch(s + 1, 1 - slot)

    wait_fetch(slot)

    qt = _row_as_columns(q_ref[pl.ds(s, 1), :] * SCALE)
    knt = _row_as_columns(kn_ref[pl.ds(s, 1), :])
    lane_head = lax.broadcasted_iota(jnp.int32, (1, WIDTH), 1) // HEAD_DIM
    pself_row = jnp.zeros((1, WIDTH), F32)
    l_row = jnp.zeros((1, WIDTH), F32)
    accs = []
    for h in range(N_HEADS):
        rows = slice(h * HEAD_DIM, (h + 1) * HEAD_DIM)
        qh = qt[rows]
        s_self = jnp.sum(qh * knt[rows], axis=0, keepdims=True)
        logits = [jnp.sum(kbuf[slot, h * SLABS + t] * qh, axis=0, keepdims=True) for t in range(SLABS)]
        mx = logits[0]
        for s_t in logits[1:]:
            mx = jnp.maximum(mx, s_t)
        m = jnp.maximum(s_self, jnp.max(mx, axis=1, keepdims=True))
        p_self = jnp.exp(s_self - m)
        l_h = p_self
        acc = jnp.zeros((HEAD_DIM, LANES), F32)
        for t, s_t in enumerate(logits):
            p = jnp.exp(s_t - m)
            l_h = l_h + jnp.sum(p, axis=1, keepdims=True)
            acc = acc + vbuf[slot, h * SLABS + t] * p
        accs.append(acc)
        pself_row = jnp.where(lane_head == h, p_self[:, 0:1], pself_row)
        l_row = jnp.where(lane_head == h, l_h[:, 0:1], l_row)
    pv = jnp.sum(jnp.concatenate(accs, axis=0).T, axis=0, keepdims=True)
    o_ref[pl.ds(s, 1), :] = (pv + pself_row * vn_ref[pl.ds(s, 1), :]) / l_row


def _sample_attn(l, sel, page_table, q, k_new, v_new, cache_kt, cache_vt):
    n = q.shape[0]
    full = pl.BlockSpec((n, WIDTH), lambda s, sel_, pt: (0, 0))
    grid_spec = pltpu.PrefetchScalarGridSpec(
        num_scalar_prefetch=2,
        grid=(n,),
        in_specs=[full, full, full, pl.BlockSpec(memory_space=pl.ANY), pl.BlockSpec(memory_space=pl.ANY)],
        out_specs=full,
        scratch_shapes=[pltpu.VMEM((2, N_SLABS, HEAD_DIM, PAGE_SIZE), F32),
                        pltpu.VMEM((2, N_SLABS, HEAD_DIM, PAGE_SIZE), F32),
                        pltpu.SemaphoreType.DMA((2, 2))])
    return pl.pallas_call(
        functools.partial(_sample_attn_kernel, l, page_table.shape[1]),
        grid_spec=grid_spec,
        out_shape=jax.ShapeDtypeStruct((n, WIDTH), F32),
        compiler_params=pltpu.CompilerParams(dimension_semantics=("arbitrary",),
                                             vmem_limit_bytes=VMEM_LIMIT),
        name="sample_attn",
    )(sel.reshape(-1), page_table.reshape(-1), q, k_new, v_new, cache_kt, cache_vt)


def _head_tile(g):
    return jnp.tile(g, N_HEADS).reshape(1, WIDTH)


def _layer(l, x_prompt, x_sample, cache_kt, cache_vt, page_table, c_prompt, c_sample,
           norm_mix_g, norm_ffn_g, w_ada, b_ada, w_in, qn_g, kn_g, vn_g, w_s, b_s,
           on_a, on_b, w_out, w1, w2):
    nb_, seq = x_prompt.shape[0], x_prompt.shape[1]
    ns = x_sample.shape[0]

    w_in_b, w_out_b = w_in[l].astype(BF16), w_out[l].astype(BF16)
    w1_b, w2_b = w1[l].astype(BF16), w2[l].astype(BF16)
    g_mix, g_ffn = norm_mix_g[l].reshape(1, D_MODEL), norm_ffn_g[l].reshape(1, D_MODEL)
    qg, kg, vg = _head_tile(qn_g[l]), _head_tile(kn_g[l]), _head_tile(vn_g[l])
    ona, onb = on_a[l].reshape(1, WIDTH), on_b[l].reshape(1, WIDTH)
    head_of = jnp.arange(2 * LANES) // HEAD_DIM
    bd = (head_of[:, None] == head_of[None, :]).astype(BF16)
    bias = jnp.repeat(b_s[l].T, HEAD_DIM, axis=1)
    ws0 = jnp.repeat(w_s[l][:, 0, 0], HEAD_DIM).reshape(1, WIDTH)
    bs0 = jnp.repeat(b_s[l][:, 0], HEAD_DIM).reshape(1, WIDTH)

    mod = _ada(jnp.concatenate([c_prompt, c_sample], axis=0), w_ada[l], b_ada[l])
    mod_p = mod[:nb_].reshape(nb_, 6, D_MODEL)
    mod_s = mod[nb_:].reshape(ns, 6, D_MODEL).transpose(1, 0, 2)

    xs = x_sample.reshape(ns, D_MODEL)
    q_s, k_s, v_s, cv_s, an_s = _sample_in(xs, mod_s, g_mix, w_in_b, bd, qg, kg, vg, ws0, bs0, ona)

    kt_p, vt_p, kbf, km, qt, vt, an_p = _prompt_in(x_prompt, mod_p, g_mix, w_in_b, bd, qg, kg, vg,
                                                   w_s[l], bias, ona)
    b_p = _prompt_attn(qt, kbf, vt, km)
    y_p, gate_parts = _prompt_out(l, x_prompt, an_p, b_p, mod_p, onb, g_ffn, w_out_b, w1_b, w2_b,
                                  q_s, page_table, cache_kt)

    sel = _sample_top3(gate_parts)
    b_s_out = _sample_attn(l, sel, page_table, q_s, k_s, v_s, cache_kt, cache_vt)
    y_s = _sample_out(xs, an_s, b_s_out, mod_s, onb, g_ffn, w_out_b, w1_b, w2_b)

    hshape = (N_HEADS, HEAD_DIM)

    def rows_from_cols(t):
        return t.reshape(nb_, *hshape, seq).transpose(0, 3, 1, 2)

    return (y_p, y_s.reshape(x_sample.shape), rows_from_cols(kt_p), rows_from_cols(vt_p),
            k_s.reshape(ns, 1, *hshape), v_s.reshape(ns, 1, *hshape), cv_s.reshape(ns, 1, *hshape))


def kernel(x_prompt, x_sample, cache_k, cache_v, page_table, c_prompt, c_sample, norm_mix_g, norm_ffn_g,
           w_ada, b_ada, w_in, qn_g, kn_g, vn_g, w_s, b_s, on_a, on_b, w_out, w1, w2):
    depth = w_in.shape[0]
    cache_kt = jnp.transpose(cache_k, (0, 1, 3, 4, 2))
    cache_vt = jnp.transpose(cache_v, (0, 1, 3, 4, 2))
    x_p, x_s = x_prompt, x_sample
    kp, vp, ks, vs, cv = [], [], [], [], []
    for l in range(depth):
        x_p, x_s, k_p, v_p, k_s, v_s, cv_s = _layer(
            l, x_p, x_s, cache_kt, cache_vt, page_table, c_prompt, c_sample, norm_mix_g, norm_ffn_g,
            w_ada, b_ada, w_in, qn_g, kn_g, vn_g, w_s, b_s, on_a, on_b, w_out, w1, w2)
        kp.append(k_p); vp.append(v_p); ks.append(k_s); vs.append(v_s); cv.append(cv_s)
    return (x_p, x_s, jnp.stack(kp), jnp.stack(vp), jnp.stack(ks), jnp.stack(vs), jnp.stack(cv))
```

```python
import functools

import jax
import jax.numpy as jnp
from jax import lax
from jax.experimental import pallas as pl
from jax.experimental.pallas import tpu as pltpu

D_MODEL = 1024
HEAD_DIM = 64
N_HEADS = 8
WIDTH = N_HEADS * HEAD_DIM
IN_COLS = 5 * WIDTH
CHUNK = 128
MOBA_BLOCK = 256
MOBA_TOPK = 3
PAGE_SIZE = 128
PAGES_PER_BLOCK = MOBA_BLOCK // PAGE_SIZE
D_FF = 4 * D_MODEL
EPS = 1e-6
NEG = -1e30
SCALE = HEAD_DIM ** -0.5
LOG2E = 1.4426950408889634

LANES = 128
SUBLANES = 8
PAIR = LANES // HEAD_DIM
N_PAIRS = N_HEADS // PAIR
LOGITS_AHEAD = 5
VMEM_LIMIT = 56 * 1024 * 1024

F32 = jnp.float32
BF16 = jnp.bfloat16


def _dot(a, b):
    return jnp.dot(a, b, preferred_element_type=F32)


def _rms_rows(x, g):
    return x * lax.rsqrt(jnp.mean(x * x, axis=-1, keepdims=True) + EPS) * g


def _head_rms(t, bd, g):
    t2 = (t * t).astype(BF16)
    half = bd.shape[0]
    ss = jnp.concatenate([_dot(t2[:, :half], bd), _dot(t2[:, half:], bd)], axis=-1)
    return t * lax.rsqrt(ss * (1.0 / HEAD_DIM) + EPS) * g


def _in_proj(x, sh, sc, g_mix, w_in):
    h = _rms_rows(x, g_mix) * (1.0 + sc) + sh
    return _dot(h.astype(BF16), w_in)


def _ada_kernel(c_ref, w_ref, b_ref, o_ref):
    s = jax.nn.silu(c_ref[...])
    o_ref[...] = jnp.dot(s, w_ref[...], preferred_element_type=F32,
                         precision=lax.Precision.HIGHEST) + b_ref[...]


def _ada(c_all, w_ada, b_ada):
    n, tn = c_all.shape[0], 512
    cols = w_ada.shape[1]
    return pl.pallas_call(
        _ada_kernel,
        grid=(cols // tn,),
        in_specs=[pl.BlockSpec((n, D_MODEL), lambda j: (0, 0)),
                  pl.BlockSpec((D_MODEL, tn), lambda j: (0, j)),
                  pl.BlockSpec((1, tn), lambda j: (0, j))],
        out_specs=pl.BlockSpec((n, tn), lambda j: (0, j)),
        out_shape=jax.ShapeDtypeStruct((n, cols), F32),
        compiler_params=pltpu.CompilerParams(dimension_semantics=("parallel",),
                                             vmem_limit_bytes=VMEM_LIMIT),
        name="ada_mod",
    )(c_all, w_ada, b_ada.reshape(1, cols))


def _prompt_in_kernel(x_ref, mod_ref, gmix_ref, win_ref, bd_ref, qg_ref, kg_ref, vg_ref,
                      ws_ref, bias_ref, ona_ref,
                      kt_ref, vtf_ref, kbf_ref, km_ref, qt_ref, vt_ref, an_ref):
    tm = x_ref.shape[1]
    j = pl.program_id(1)
    z = _in_proj(x_ref[0], mod_ref[0, 0:1, :], mod_ref[0, 1:2, :], gmix_ref[...], win_ref[...])
    bd = bd_ref[...]
    ua, va = z[:, 0:WIDTH], z[:, WIDTH:2 * WIDTH]
    q, k, v = z[:, 2 * WIDTH:3 * WIDTH], z[:, 3 * WIDTH:4 * WIDTH], z[:, 4 * WIDTH:5 * WIDTH]

    k_n = _head_rms(k, bd, kg_ref[...])
    kt_ref[0] = k_n.T
    kbf_ref[0] = k_n.astype(BF16)
    km_ref[0, pl.ds(j, 1), :] = jnp.mean(k_n, axis=0, keepdims=True)
    v_t = v.T
    vtf_ref[0] = v_t
    qt_ref[0, 0] = _head_rms(q, bd, qg_ref[...]).T
    vt_ref[0, 0] = v_t.astype(BF16)

    vn = _head_rms(jax.nn.gelu(va), bd, vg_ref[...]).astype(BF16)
    row = lax.broadcasted_iota(jnp.int32, (CHUNK, CHUNK), 0)
    col = lax.broadcasted_iota(jnp.int32, (CHUNK, CHUNK), 1)
    ws = [jnp.where(row >= col, ws_ref[h], 0.0).astype(BF16) for h in range(N_HEADS)]
    ws_pair = [jnp.concatenate(ws[PAIR * p:PAIR * (p + 1)], axis=1) for p in range(N_PAIRS)]
    lane = lax.broadcasted_iota(jnp.int32, (CHUNK, LANES), 1)
    zero = jnp.zeros((CHUNK, LANES), BF16)
    chunks = []
    for c in range(tm // CHUNK):
        pieces = []
        for p in range(N_PAIRS):
            vp = vn[c * CHUNK:(c + 1) * CHUNK, p * LANES:(p + 1) * LANES]
            by_head = [jnp.where(lane // HEAD_DIM == hh, vp, zero) for hh in range(PAIR)]
            pieces.append(_dot(ws_pair[p], jnp.concatenate(by_head, axis=0)))
        chunks.append(jnp.concatenate(pieces, axis=-1) + bias_ref[...])
    mixed = jnp.concatenate(chunks, axis=0)
    an_ref[0] = _rms_rows(jax.nn.gelu(ua) * mixed, ona_ref[...]).astype(BF16)


def _prompt_in(x, mod, g_mix, w_in, bd, qg, kg, vg, w_s, bias, on_a):
    b, l, _ = x.shape
    tm = MOBA_BLOCK
    nt = l // tm
    const2 = lambda i, j: (0, 0)
    rows = pl.BlockSpec((1, tm, WIDTH), lambda i, j: (i, j, 0))
    cols = pl.BlockSpec((1, 1, WIDTH, tm), lambda i, j: (i, j, 0, 0))
    full_cols = pl.BlockSpec((1, WIDTH, tm), lambda i, j: (i, 0, j))
    return pl.pallas_call(
        _prompt_in_kernel,
        grid=(b, nt),
        in_specs=[pl.BlockSpec((1, tm, D_MODEL), lambda i, j: (i, j, 0)),
                  pl.BlockSpec((1, 6, D_MODEL), lambda i, j: (i, 0, 0)),
                  pl.BlockSpec((1, D_MODEL), const2),
                  pl.BlockSpec((D_MODEL, IN_COLS), const2),
                  pl.BlockSpec(bd.shape, const2),
                  pl.BlockSpec((1, WIDTH), const2),
                  pl.BlockSpec((1, WIDTH), const2),
                  pl.BlockSpec((1, WIDTH), const2),
                  pl.BlockSpec((N_HEADS, CHUNK, CHUNK), lambda i, j: (0, 0, 0)),
                  pl.BlockSpec((CHUNK, WIDTH), const2),
                  pl.BlockSpec((1, WIDTH), const2)],
        out_specs=[full_cols, full_cols, rows,
                   pl.BlockSpec((1, nt, WIDTH), lambda i, j: (i, 0, 0)),
                   cols, cols, rows],
        out_shape=[jax.ShapeDtypeStruct((b, WIDTH, l), F32),
                   jax.ShapeDtypeStruct((b, WIDTH, l), F32),
                   jax.ShapeDtypeStruct((b, l, WIDTH), BF16),
                   jax.ShapeDtypeStruct((b, nt, WIDTH), F32),
                   jax.ShapeDtypeStruct((b, nt, WIDTH, tm), F32),
                   jax.ShapeDtypeStruct((b, nt, WIDTH, tm), BF16),
                   jax.ShapeDtypeStruct((b, l, WIDTH), BF16)],
        compiler_params=pltpu.CompilerParams(dimension_semantics=("parallel", "arbitrary"),
                                             vmem_limit_bytes=VMEM_LIMIT),
        name="prompt_in",
    )(x, mod, g_mix, w_in, bd, qg, kg, vg, w_s, bias, on_a)


def _prompt_attn_kernel(qt_ref, k_ref, vt_ref, km_ref, o_ref, qs_ref, sel_ref, m_ref, l_ref, acc_ref):
    qb = pl.program_id(1)
    nb = km_ref.shape[1]
    blk = MOBA_BLOCK

    qt = qt_ref[0, 0]
    km = km_ref[0]
    lane = lax.broadcasted_iota(jnp.int32, (nb, LANES), 1)
    drow = lax.broadcasted_iota(jnp.int32, (LANES, blk), 0)
    bid = lax.broadcasted_iota(jnp.int32, (nb, blk), 0)
    for h in range(N_HEADS):
        p, hh = divmod(h, PAIR)
        lo = hh * HEAD_DIM
        qp = qt[p * LANES:(p + 1) * LANES]
        qs_ref[h] = jnp.where((drow >= lo) & (drow < lo + HEAD_DIM), qp * (SCALE * LOG2E), 0.0).astype(BF16)
        kmm = jnp.where((lane >= lo) & (lane < lo + HEAD_DIM), km[:, p * LANES:(p + 1) * LANES], 0.0)
        gate = jnp.dot(kmm, qp, preferred_element_type=F32, precision=lax.Precision.HIGHEST)
        cnt = jnp.zeros((nb, blk), jnp.int32)
        for m in range(nb):
            gm = gate[m:m + 1, :]
            beats = (gm > gate) | ((gm == gate) & (bid > m))
            cnt = cnt + jnp.where(beats & (qb > m), 1, 0)
        sel_ref[h] = jnp.where((cnt < MOBA_TOPK) & (bid < qb), 1.0, 0.0)

    def step(own, kb):
        k = k_ref[0, pl.ds(pl.multiple_of(kb * blk, blk), blk), :]
        if own:
            key_i = lax.broadcasted_iota(jnp.int32, (blk, blk), 0)
            qry_i = lax.broadcasted_iota(jnp.int32, (blk, blk), 1)
            causal = key_i <= qry_i

        def logits(h):
            p = h // PAIR
            return _dot(k[:, p * LANES:(p + 1) * LANES], qs_ref[h])

        ahead = [logits(h) for h in range(LOGITS_AHEAD)]
        for h in range(N_HEADS):
            rows = slice(h * HEAD_DIM, (h + 1) * HEAD_DIM)
            s = ahead.pop(0)
            if h + LOGITS_AHEAD < N_HEADS:
                ahead.append(logits(h + LOGITS_AHEAD))
            vt = vt_ref[0, kb, rows, :]
            if own:
                s = jnp.where(causal, s, NEG)
                m_new = jnp.max(s, axis=0, keepdims=True)
                pr = jnp.exp2(s - m_new)
                l_ref[h] = jnp.sum(pr, axis=0, keepdims=True)
                acc_ref[rows, :] = _dot(vt, pr.astype(BF16))
            else:
                taken = sel_ref[h, pl.ds(kb, 1), :] > 0.0
                m_i = m_ref[h]
                m_new = jnp.where(taken, jnp.maximum(m_i, jnp.max(s, axis=0, keepdims=True)), m_i)
                alpha = jnp.exp2(m_i - m_new)
                pr = jnp.exp2(s - jnp.where(taken, m_new, -NEG))
                l_ref[h] = alpha * l_ref[h] + jnp.sum(pr, axis=0, keepdims=True)
                acc_ref[rows, :] = alpha * acc_ref[rows, :] + _dot(vt, pr.astype(BF16))
            m_ref[h] = m_new

    step(True, qb)

    def past(kb, carry):
        step(False, kb)
        return carry

    lax.fori_loop(0, qb, past, 0)
    outs = [acc_ref[h * HEAD_DIM:(h + 1) * HEAD_DIM, :] / l_ref[h] for h in range(N_HEADS)]
    o_ref[0] = jnp.concatenate(outs, axis=0).T


def _prompt_attn(qt, kbf, vt, km):
    b, nb, _, blk = qt.shape
    l = nb * blk
    return pl.pallas_call(
        _prompt_attn_kernel,
        grid=(b, nb),
        in_specs=[pl.BlockSpec((1, 1, WIDTH, blk), lambda i, j: (i, j, 0, 0)),
                  pl.BlockSpec((1, l, WIDTH), lambda i, j: (i, 0, 0)),
                  pl.BlockSpec((1, nb, WIDTH, blk), lambda i, j: (i, 0, 0, 0)),
                  pl.BlockSpec((1, nb, WIDTH), lambda i, j: (i, 0, 0))],
        out_specs=pl.BlockSpec((1, blk, WIDTH), lambda i, j: (i, j, 0)),
        out_shape=jax.ShapeDtypeStruct((b, l, WIDTH), F32),
        scratch_shapes=[pltpu.VMEM((N_HEADS, LANES, blk), BF16),
                        pltpu.VMEM((N_HEADS, nb, blk), F32),
                        pltpu.VMEM((N_HEADS, 1, blk), F32),
                        pltpu.VMEM((N_HEADS, 1, blk), F32),
                        pltpu.VMEM((WIDTH, blk), F32)],
        compiler_params=pltpu.CompilerParams(dimension_semantics=("parallel", "parallel"),
                                             vmem_limit_bytes=VMEM_LIMIT),
        name="prompt_attn",
    )(qt, kbf, vt, km)


def _out_mlp(x, a_n, b_out, sh_f, sc_f, g_m, g_f, onb, gffn, wout_ref, w1_ref, w2_ref):
    b_n = _rms_rows(b_out, onb).astype(BF16)
    mixed = _dot(a_n, wout_ref[0:WIDTH, :]) + _dot(b_n, wout_ref[WIDTH:2 * WIDTH, :])
    x1 = x + g_m * mixed
    h2 = (_rms_rows(x1, gffn) * (1.0 + sc_f) + sh_f).astype(BF16)
    acc = jnp.zeros_like(x1)
    for c in range(D_FF // D_MODEL):
        t = jnp.maximum(_dot(h2, w1_ref[:, c * D_MODEL:(c + 1) * D_MODEL]), 0.0)
        acc = acc + _dot((t * t).astype(BF16), w2_ref[c * D_MODEL:(c + 1) * D_MODEL, :])
    return x1 + g_f * acc


DIM_GROUPS = HEAD_DIM // SUBLANES


def _block_gate_partials(q_row, page_refs):
    qt = _row_as_columns(q_row)
    n_blocks = len(page_refs) // PAGES_PER_BLOCK
    sub = lax.broadcasted_iota(jnp.int32, (N_HEADS, LANES), 0)
    blocks = [jnp.zeros((N_HEADS, LANES), F32)] * n_blocks
    for h in range(N_HEADS):
        tot = [None] * n_blocks
        for g in range(DIM_GROUPS):
            rows = slice(g * SUBLANES, (g + 1) * SUBLANES)
            qv = qt[h * HEAD_DIM + g * SUBLANES:h * HEAD_DIM + (g + 1) * SUBLANES, :]
            for t in range(n_blocks):
                pages = sum(page_refs[PAGES_PER_BLOCK * t + u][0, 0, h, rows, :] for u in range(1, PAGES_PER_BLOCK))
                term = (page_refs[PAGES_PER_BLOCK * t][0, 0, h, rows, :] + pages) * qv
                tot[t] = term if tot[t] is None else tot[t] + term
        for t in range(n_blocks):
            blocks[t] = jnp.where(sub == h, jnp.sum(tot[t], axis=0, keepdims=True), blocks[t])
    return jnp.stack(blocks, axis=0)


def _prompt_out_kernel(steps_per_seq, pt_ref, x_ref, an_ref, b_ref, mod_ref, onb_ref, gffn_ref,
                       wout_ref, w1_ref, w2_ref, qs_ref, *refs):
    page_refs, (y_ref, g_ref) = refs[:-2], refs[-2:]
    step = pl.program_id(0) * pl.num_programs(1) + pl.program_id(1)
    y_ref[0] = _out_mlp(x_ref[0], an_ref[0], b_ref[0],
                        mod_ref[0, 3:4, :], mod_ref[0, 4:5, :], mod_ref[0, 2:3, :], mod_ref[0, 5:6, :],
                        onb_ref[...], gffn_ref[...], wout_ref, w1_ref, w2_ref)
    g_ref[0] = _block_gate_partials(qs_ref[pl.ds(step // steps_per_seq, 1), :], page_refs)


def _prompt_out(l_idx, x, a_n, b_out, mod, on_b, g_ffn, w_out, w1, w2, q_s, page_table, cache_kt, tm=256):
    b, l, _ = x.shape
    nt = l // tm
    ns, n_pages = page_table.shape
    steps_per_seq = (b * nt) // ns
    pages_per_step = n_pages // steps_per_seq
    blocks_per_step = pages_per_step // PAGES_PER_BLOCK
    assert steps_per_seq * ns == b * nt and pages_per_step * steps_per_seq == n_pages
    assert blocks_per_step * PAGES_PER_BLOCK == pages_per_step

    def page_map(t):
        return lambda i, j, pt: (l_idx, pt[(i * nt + j) * pages_per_step + t], 0, 0, 0)

    const2 = lambda i, j, pt: (0, 0)
    rows = lambda i, j, pt: (i, j, 0)
    once = pl.Buffered(1)
    gate_block = (blocks_per_step, N_HEADS, LANES)
    grid_spec = pltpu.PrefetchScalarGridSpec(
        num_scalar_prefetch=1,
        grid=(b, nt),
        in_specs=[pl.BlockSpec((1, tm, D_MODEL), rows),
                  pl.BlockSpec((1, tm, WIDTH), rows),
                  pl.BlockSpec((1, tm, WIDTH), rows),
                  pl.BlockSpec((1, 6, D_MODEL), lambda i, j, pt: (i, 0, 0)),
                  pl.BlockSpec((1, WIDTH), const2),
                  pl.BlockSpec((1, D_MODEL), const2),
                  pl.BlockSpec((2 * WIDTH, D_MODEL), const2, pipeline_mode=once),
                  pl.BlockSpec((D_MODEL, D_FF), const2, pipeline_mode=once),
                  pl.BlockSpec((D_FF, D_MODEL), const2, pipeline_mode=once),
                  pl.BlockSpec((ns, WIDTH), const2)]
                 + [pl.BlockSpec((1, 1, N_HEADS, HEAD_DIM, PAGE_SIZE), page_map(t)) for t in range(pages_per_step)],
        out_specs=[pl.BlockSpec((1, tm, D_MODEL), rows),
                   pl.BlockSpec((1,) + gate_block, lambda i, j, pt: (i * nt + j, 0, 0, 0))])
    y, g = pl.pallas_call(
        functools.partial(_prompt_out_kernel, steps_per_seq),
        grid_spec=grid_spec,
        out_shape=[jax.ShapeDtypeStruct(x.shape, F32),
                   jax.ShapeDtypeStruct((b * nt,) + gate_block, F32)],
        compiler_params=pltpu.CompilerParams(dimension_semantics=("parallel", "parallel"),
                                             vmem_limit_bytes=VMEM_LIMIT),
        name="prompt_out",
    )(page_table.reshape(-1), x, a_n, b_out, mod, on_b, g_ffn, w_out, w1, w2, q_s,
      *([cache_kt] * pages_per_step))
    return y, g.reshape(ns, n_pages // PAGES_PER_BLOCK, N_HEADS, LANES)


def _sample_out_kernel(x_ref, an_ref, b_ref, mod_ref, onb_ref, gffn_ref, wout_ref, w1_ref, w2_ref, y_ref):
    y_ref[...] = _out_mlp(x_ref[...], an_ref[...], b_ref[...],
                          mod_ref[3], mod_ref[4], mod_ref[2], mod_ref[5],
                          onb_ref[...], gffn_ref[...], wout_ref, w1_ref, w2_ref)


def _sample_out(x, a_n, b_out, mod_t, on_b, g_ffn, w_out, w1, w2):
    return pl.pallas_call(
        _sample_out_kernel,
        out_shape=jax.ShapeDtypeStruct(x.shape, F32),
        compiler_params=pltpu.CompilerParams(vmem_limit_bytes=VMEM_LIMIT),
        name="sample_out",
    )(x, a_n, b_out, mod_t, on_b, g_ffn, w_out, w1, w2)


def _sample_in_kernel(x_ref, mod_ref, gmix_ref, win_ref, bd_ref, qg_ref, kg_ref, vg_ref,
                      ws0_ref, bs0_ref, ona_ref,
                      q_ref, k_ref, v_ref, cv_ref, an_ref):
    z = _in_proj(x_ref[...], mod_ref[0], mod_ref[1], gmix_ref[...], win_ref[...])
    bd = bd_ref[...]
    ua, va = z[:, 0:WIDTH], z[:, WIDTH:2 * WIDTH]
    q, k, v = z[:, 2 * WIDTH:3 * WIDTH], z[:, 3 * WIDTH:4 * WIDTH], z[:, 4 * WIDTH:5 * WIDTH]
    q_ref[...] = _head_rms(q, bd, qg_ref[...])
    k_ref[...] = _head_rms(k, bd, kg_ref[...])
    v_ref[...] = v
    vn = _head_rms(jax.nn.gelu(va), bd, vg_ref[...])
    cv_ref[...] = vn
    mixed = ws0_ref[...] * vn + bs0_ref[...]
    an_ref[...] = _rms_rows(jax.nn.gelu(ua) * mixed, ona_ref[...]).astype(BF16)


def _sample_in(x, mod_t, g_mix, w_in, bd, qg, kg, vg, ws0, bs0, on_a):
    n = x.shape[0]
    row = jax.ShapeDtypeStruct((n, WIDTH), F32)
    return pl.pallas_call(
        _sample_in_kernel,
        out_shape=[row, row, row, row, jax.ShapeDtypeStruct((n, WIDTH), BF16)],
        compiler_params=pltpu.CompilerParams(vmem_limit_bytes=VMEM_LIMIT),
        name="sample_in",
    )(x, mod_t, g_mix, w_in, bd, qg, kg, vg, ws0, bs0, on_a)


def _row_as_columns(row):
    return jnp.broadcast_to(row, (LANES, WIDTH)).T


def _sample_top3_kernel(g_ref, sel_ref):
    nb = g_ref.shape[1]
    gate = jnp.sum(g_ref[0], axis=2, keepdims=True) * (1.0 / MOBA_BLOCK)
    bidf = lax.broadcasted_iota(jnp.int32, gate.shape, 0).astype(F32)
    picks = []
    for _ in range(MOBA_TOPK):
        mx = jnp.max(gate, axis=0, keepdims=True)
        idx = jnp.min(jnp.where(gate == mx, bidf, float(nb)), axis=0, keepdims=True)
        picks.append(idx)
        gate = jnp.where(bidf == idx, -jnp.inf, gate)
    sel = jnp.concatenate(picks, axis=0)
    sel_ref[0] = jnp.broadcast_to(sel, sel_ref.shape[1:]).astype(jnp.int32)


def _sample_top3(g):
    n = g.shape[0]
    sel_block = (MOBA_TOPK, N_HEADS, LANES)
    sel = pl.pallas_call(
        _sample_top3_kernel,
        grid=(n,),
        in_specs=[pl.BlockSpec((1,) + g.shape[1:], lambda s: (s, 0, 0, 0))],
        out_specs=pl.BlockSpec((1,) + sel_block, lambda s: (s, 0, 0, 0)),
        out_shape=jax.ShapeDtypeStruct((n,) + sel_block, jnp.int32),
        compiler_params=pltpu.CompilerParams(dimension_semantics=("parallel",),
                                             vmem_limit_bytes=VMEM_LIMIT),
        name="sample_top3",
    )(g)
    return sel[:, :, :, 0]


SLABS = MOBA_TOPK * PAGES_PER_BLOCK
N_SLABS = N_HEADS * SLABS


def _sample_attn_kernel(l, n_pages, sel_ref, pt_ref, q_ref, kn_ref, vn_ref, ck_ref, cv_ref, o_ref,
                        kbuf, vbuf, sem):
    s = pl.program_id(0)
    n = pl.num_programs(0)
    slot = lax.rem(s, 2)

    def slab_copies(page, h, i, slot_):
        return (pltpu.make_async_copy(ck_ref.at[l, page, h], kbuf.at[slot_, i], sem.at[0, slot_]),
                pltpu.make_async_copy(cv_ref.at[l, page, h], vbuf.at[slot_, i], sem.at[1, slot_]))

    def start_fetch(seq, slot_):
        for h in range(N_HEADS):
            for r in range(MOBA_TOPK):
                blk = sel_ref[(seq * MOBA_TOPK + r) * N_HEADS + h]
                for u in range(PAGES_PER_BLOCK):
                    page = pt_ref[seq * n_pages + blk * PAGES_PER_BLOCK + u]
                    for c in slab_copies(page, h, h * SLABS + r * PAGES_PER_BLOCK + u, slot_):
                        c.start()

    def wait_fetch(slot_):
        for i in range(N_SLABS):
            for c in slab_copies(0, 0, i, slot_):
                c.wait()

    @pl.when(s == 0)
    def _():
        start_fetch(0, 0)

    @pl.when(s + 1 < n)
    def _():
        start_fetch(s + 1, 1 - slot)

    wait_fetch(slot)

    qt = _row_as_columns(q_ref[pl.ds(s, 1), :] * SCALE)
    knt = _row_as_columns(kn_ref[pl.ds(s, 1), :])
    lane_head = lax.broadcasted_iota(jnp.int32, (1, WIDTH), 1) // HEAD_DIM
    pself_row = jnp.zeros((1, WIDTH), F32)
    l_row = jnp.zeros((1, WIDTH), F32)
    accs = []
    for h in range(N_HEADS):
        rows = slice(h * HEAD_DIM, (h + 1) * HEAD_DIM)
        qh = qt[rows]
        s_self = jnp.sum(qh * knt[rows], axis=0, keepdims=True)
        logits = [jnp.sum(kbuf[slot, h * SLABS + t] * qh, axis=0, keepdims=True) for t in range(SLABS)]
        mx = logits[0]
        for s_t in logits[1:]:
            mx = jnp.maximum(mx, s_t)
        m = jnp.maximum(s_self, jnp.max(mx, axis=1, keepdims=True))
        p_self = jnp.exp(s_self - m)
        l_h = p_self
        acc = jnp.zeros((HEAD_DIM, LANES), F32)
        for t, s_t in enumerate(logits):
            p = jnp.exp(s_t - m)
            l_h = l_h + jnp.sum(p, axis=1, keepdims=True)
            acc = acc + vbuf[slot, h * SLABS + t] * p
        accs.append(acc)
        pself_row = jnp.where(lane_head == h, p_self[:, 0:1], pself_row)
        l_row = jnp.where(lane_head == h, l_h[:, 0:1], l_row)
    pv = jnp.sum(jnp.concatenate(accs, axis=0).T, axis=0, keepdims=True)
    o_ref[pl.ds(s, 1), :] = (pv + pself_row * vn_ref[pl.ds(s, 1), :]) / l_row


def _sample_attn(l, sel, page_table, q, k_new, v_new, cache_kt, cache_vt):
    n = q.shape[0]
    full = pl.BlockSpec((n, WIDTH), lambda s, sel_, pt: (0, 0))
    grid_spec = pltpu.PrefetchScalarGridSpec(
        num_scalar_prefetch=2,
        grid=(n,),
        in_specs=[full, full, full, pl.BlockSpec(memory_space=pl.ANY), pl.BlockSpec(memory_space=pl.ANY)],
        out_specs=full,
        scratch_shapes=[pltpu.VMEM((2, N_SLABS, HEAD_DIM, PAGE_SIZE), F32),
                        pltpu.VMEM((2, N_SLABS, HEAD_DIM, PAGE_SIZE), F32),
                        pltpu.SemaphoreType.DMA((2, 2))])
    return pl.pallas_call(
        functools.partial(_sample_attn_kernel, l, page_table.shape[1]),
        grid_spec=grid_spec,
        out_shape=jax.ShapeDtypeStruct((n, WIDTH), F32),
        compiler_params=pltpu.CompilerParams(dimension_semantics=("arbitrary",),
                                             vmem_limit_bytes=VMEM_LIMIT),
        name="sample_attn",
    )(sel.reshape(-1), page_table.reshape(-1), q, k_new, v_new, cache_kt, cache_vt)


def _head_tile(g):
    return jnp.tile(g, N_HEADS).reshape(1, WIDTH)


def _layer(l, x_prompt, x_sample, cache_kt, cache_vt, page_table, c_prompt, c_sample,
           norm_mix_g, norm_ffn_g, w_ada, b_ada, w_in, qn_g, kn_g, vn_g, w_s, b_s,
           on_a, on_b, w_out, w1, w2):
    nb_, seq = x_prompt.shape[0], x_prompt.shape[1]
    ns = x_sample.shape[0]

    w_in_b, w_out_b = w_in[l].astype(BF16), w_out[l].astype(BF16)
    w1_b, w2_b = w1[l].astype(BF16), w2[l].astype(BF16)
    g_mix, g_ffn = norm_mix_g[l].reshape(1, D_MODEL), norm_ffn_g[l].reshape(1, D_MODEL)
    qg, kg, vg = _head_tile(qn_g[l]), _head_tile(kn_g[l]), _head_tile(vn_g[l])
    ona, onb = on_a[l].reshape(1, WIDTH), on_b[l].reshape(1, WIDTH)
    head_of = jnp.arange(2 * LANES) // HEAD_DIM
    bd = (head_of[:, None] == head_of[None, :]).astype(BF16)
    bias = jnp.repeat(b_s[l].T, HEAD_DIM, axis=1)
    ws0 = jnp.repeat(w_s[l][:, 0, 0], HEAD_DIM).reshape(1, WIDTH)
    bs0 = jnp.repeat(b_s[l][:, 0], HEAD_DIM).reshape(1, WIDTH)

    mod = _ada(jnp.concatenate([c_prompt, c_sample], axis=0), w_ada[l], b_ada[l])
    mod_p = mod[:nb_].reshape(nb_, 6, D_MODEL)
    mod_s = mod[nb_:].reshape(ns, 6, D_MODEL).transpose(1, 0, 2)

    xs = x_sample.reshape(ns, D_MODEL)
    q_s, k_s, v_s, cv_s, an_s = _sample_in(xs, mod_s, g_mix, w_in_b, bd, qg, kg, vg, ws0, bs0, ona)

    kt_p, vt_p, kbf, km, qt, vt, an_p = _prompt_in(x_prompt, mod_p, g_mix, w_in_b, bd, qg, kg, vg,
                                                   w_s[l], bias, ona)
    b_p = _prompt_attn(qt, kbf, vt, km)
    y_p, gate_parts = _prompt_out(l, x_prompt, an_p, b_p, mod_p, onb, g_ffn, w_out_b, w1_b, w2_b,
                                  q_s, page_table, cache_kt)

    sel = _sample_top3(gate_parts)
    b_s_out = _sample_attn(l, sel, page_table, q_s, k_s, v_s, cache_kt, cache_vt)
    y_s = _sample_out(xs, an_s, b_s_out, mod_s, onb, g_ffn, w_out_b, w1_b, w2_b)

    hshape = (N_HEADS, HEAD_DIM)

    def rows_from_cols(t):
        return t.reshape(nb_, *hshape, seq).transpose(0, 3, 1, 2)

    return (y_p, y_s.reshape(x_sample.shape), rows_from_cols(kt_p), rows_from_cols(vt_p),
            k_s.reshape(ns, 1, *hshape), v_s.reshape(ns, 1, *hshape), cv_s.reshape(ns, 1, *hshape))


def kernel(x_prompt, x_sample, cache_k, cache_v, page_table, c_prompt, c_sample, norm_mix_g, norm_ffn_g,
           w_ada, b_ada, w_in, qn_g, kn_g, vn_g, w_s, b_s, on_a, on_b, w_out, w1, w2):
    depth = w_in.shape[0]
    cache_kt = jnp.transpose(cache_k, (0, 1, 3, 4, 2))
    cache_vt = jnp.transpose(cache_v, (0, 1, 3, 4, 2))
    x_p, x_s = x_prompt, x_sample
    kp, vp, ks, vs, cv = [], [], [], [], []
    for l in range(depth):
        x_p, x_s, k_p, v_p, k_s, v_s, cv_s = _layer(
            l, x_p, x_s, cache_kt, cache_vt, page_table, c_prompt, c_sample, norm_mix_g, norm_ffn_g,
            w_ada, b_ada, w_in, qn_g, kn_g, vn_g, w_s, b_s, on_a, on_b, w_out, w1, w2)
        kp.append(k_p); vp.append(v_p); ks.append(k_s); vs.append(v_s); cv.append(cv_s)
    return (x_p, x_s, jnp.stack(kp), jnp.stack(vp), jnp.stack(ks), jnp.stack(vs), jnp.stack(cv))
```

```python
import functools

import jax
import jax.numpy as jnp
from jax import lax
from jax.experimental import pallas as pl
from jax.experimental.pallas import tpu as pltpu

D_MODEL = 1024
HEAD_DIM = 64
N_HEADS = 8
WIDTH = N_HEADS * HEAD_DIM
IN_COLS = 5 * WIDTH
CHUNK = 128
MOBA_BLOCK = 256
MOBA_TOPK = 3
PAGE_SIZE = 128
PAGES_PER_BLOCK = MOBA_BLOCK // PAGE_SIZE
D_FF = 4 * D_MODEL
EPS = 1e-6
NEG = -1e30
SCALE = HEAD_DIM ** -0.5
LOG2E = 1.4426950408889634

LANES = 128
SUBLANES = 8
PAIR = LANES // HEAD_DIM
N_PAIRS = N_HEADS // PAIR
LOGITS_AHEAD = 5
VMEM_LIMIT = 56 * 1024 * 1024

F32 = jnp.float32
BF16 = jnp.bfloat16


def _dot(a, b):
    return jnp.dot(a, b, preferred_element_type=F32)


def _rms_rows(x, g):
    return x * lax.rsqrt(jnp.mean(x * x, axis=-1, keepdims=True) + EPS) * g


def _head_rms(t, bd, g):
    t2 = (t * t).astype(BF16)
    half = bd.shape[0]
    ss = jnp.concatenate([_dot(t2[:, :half], bd), _dot(t2[:, half:], bd)], axis=-1)
    return t * lax.rsqrt(ss * (1.0 / HEAD_DIM) + EPS) * g


def _in_proj(x, sh, sc, g_mix, w_in):
    h = _rms_rows(x, g_mix) * (1.0 + sc) + sh
    return _dot(h.astype(BF16), w_in)


def _ada_kernel(c_ref, w_ref, b_ref, o_ref):
    s = jax.nn.silu(c_ref[...])
    o_ref[...] = jnp.dot(s, w_ref[...], preferred_element_type=F32,
                         precision=lax.Precision.HIGHEST) + b_ref[...]


def _ada(c_all, w_ada, b_ada):
    n, tn = c_all.shape[0], 512
    cols = w_ada.shape[1]
    return pl.pallas_call(
        _ada_kernel,
        grid=(cols // tn,),
        in_specs=[pl.BlockSpec((n, D_MODEL), lambda j: (0, 0)),
                  pl.BlockSpec((D_MODEL, tn), lambda j: (0, j)),
                  pl.BlockSpec((1, tn), lambda j: (0, j))],
        out_specs=pl.BlockSpec((n, tn), lambda j: (0, j)),
        out_shape=jax.ShapeDtypeStruct((n, cols), F32),
        compiler_params=pltpu.CompilerParams(dimension_semantics=("parallel",),
                                             vmem_limit_bytes=VMEM_LIMIT),
        name="ada_mod",
    )(c_all, w_ada, b_ada.reshape(1, cols))


def _prompt_in_kernel(x_ref, mod_ref, gmix_ref, win_ref, bd_ref, qg_ref, kg_ref, vg_ref,
                      ws_ref, bias_ref, ona_ref,
                      kt_ref, vtf_ref, kbf_ref, km_ref, qt_ref, vt_ref, an_ref):
    tm = x_ref.shape[1]
    j = pl.program_id(1)
    z = _in_proj(x_ref[0], mod_ref[0, 0:1, :], mod_ref[0, 1:2, :], gmix_ref[...], win_ref[...])
    bd = bd_ref[...]
    ua, va = z[:, 0:WIDTH], z[:, WIDTH:2 * WIDTH]
    q, k, v = z[:, 2 * WIDTH:3 * WIDTH], z[:, 3 * WIDTH:4 * WIDTH], z[:, 4 * WIDTH:5 * WIDTH]

    k_n = _head_rms(k, bd, kg_ref[...])
    kt_ref[0] = k_n.T
    kbf_ref[0] = k_n.astype(BF16)
    km_ref[0, pl.ds(j, 1), :] = jnp.mean(k_n, axis=0, keepdims=True)
    v_t = v.T
    vtf_ref[0] = v_t
    qt_ref[0, 0] = _head_rms(q, bd, qg_ref[...]).T
    vt_ref[0, 0] = v_t.astype(BF16)

    vn = _head_rms(jax.nn.gelu(va), bd, vg_ref[...]).astype(BF16)
    row = lax.broadcasted_iota(jnp.int32, (CHUNK, CHUNK), 0)
    col = lax.broadcasted_iota(jnp.int32, (CHUNK, CHUNK), 1)
    ws = [jnp.where(row >= col, ws_ref[h], 0.0).astype(BF16) for h in range(N_HEADS)]
    ws_pair = [jnp.concatenate(ws[PAIR * p:PAIR * (p + 1)], axis=1) for p in range(N_PAIRS)]
    lane = lax.broadcasted_iota(jnp.int32, (CHUNK, LANES), 1)
    zero = jnp.zeros((CHUNK, LANES), BF16)
    chunks = []
    for c in range(tm // CHUNK):
        pieces = []
        for p in range(N_PAIRS):
            vp = vn[c * CHUNK:(c + 1) * CHUNK, p * LANES:(p + 1) * LANES]
            by_head = [jnp.where(lane // HEAD_DIM == hh, vp, zero) for hh in range(PAIR)]
            pieces.append(_dot(ws_pair[p], jnp.concatenate(by_head, axis=0)))
        chunks.append(jnp.concatenate(pieces, axis=-1) + bias_ref[...])
    mixed = jnp.concatenate(chunks, axis=0)
    an_ref[0] = _rms_rows(jax.nn.gelu(ua) * mixed, ona_ref[...]).astype(BF16)


def _prompt_in(x, mod, g_mix, w_in, bd, qg, kg, vg, w_s, bias, on_a):
    b, l, _ = x.shape
    tm = MOBA_BLOCK
    nt = l // tm
    const2 = lambda i, j: (0, 0)
    rows = pl.BlockSpec((1, tm, WIDTH), lambda i, j: (i, j, 0))
    cols = pl.BlockSpec((1, 1, WIDTH, tm), lambda i, j: (i, j, 0, 0))
    full_cols = pl.BlockSpec((1, WIDTH, tm), lambda i, j: (i, 0, j))
    return pl.pallas_call(
        _prompt_in_kernel,
        grid=(b, nt),
        in_specs=[pl.BlockSpec((1, tm, D_MODEL), lambda i, j: (i, j, 0)),
                  pl.BlockSpec((1, 6, D_MODEL), lambda i, j: (i, 0, 0)),
                  pl.BlockSpec((1, D_MODEL), const2),
                  pl.BlockSpec((D_MODEL, IN_COLS), const2),
                  pl.BlockSpec(bd.shape, const2),
                  pl.BlockSpec((1, WIDTH), const2),
                  pl.BlockSpec((1, WIDTH), const2),
                  pl.BlockSpec((1, WIDTH), const2),
                  pl.BlockSpec((N_HEADS, CHUNK, CHUNK), lambda i, j: (0, 0, 0)),
                  pl.BlockSpec((CHUNK, WIDTH), const2),
                  pl.BlockSpec((1, WIDTH), const2)],
        out_specs=[full_cols, full_cols, rows,
                   pl.BlockSpec((1, nt, WIDTH), lambda i, j: (i, 0, 0)),
                   cols, cols, rows],
        out_shape=[jax.ShapeDtypeStruct((b, WIDTH, l), F32),
                   jax.ShapeDtypeStruct((b, WIDTH, l), F32),
                   jax.ShapeDtypeStruct((b, l, WIDTH), BF16),
                   jax.ShapeDtypeStruct((b, nt, WIDTH), F32),
                   jax.ShapeDtypeStruct((b, nt, WIDTH, tm), F32),
                   jax.ShapeDtypeStruct((b, nt, WIDTH, tm), BF16),
                   jax.ShapeDtypeStruct((b, l, WIDTH), BF16)],
        compiler_params=pltpu.CompilerParams(dimension_semantics=("parallel", "arbitrary"),
                                             vmem_limit_bytes=VMEM_LIMIT),
        name="prompt_in",
    )(x, mod, g_mix, w_in, bd, qg, kg, vg, w_s, bias, on_a)


def _prompt_attn_kernel(qt_ref, k_ref, vt_ref, km_ref, o_ref, qs_ref, sel_ref, m_ref, l_ref, acc_ref):
    qb = pl.program_id(1)
    nb = km_ref.shape[1]
    blk = MOBA_BLOCK

    qt = qt_ref[0, 0]
    km = km_ref[0]
    lane = lax.broadcasted_iota(jnp.int32, (nb, LANES), 1)
    drow = lax.broadcasted_iota(jnp.int32, (LANES, blk), 0)
    bid = lax.broadcasted_iota(jnp.int32, (nb, blk), 0)
    for h in range(N_HEADS):
        p, hh = divmod(h, PAIR)
        lo = hh * HEAD_DIM
        qp = qt[p * LANES:(p + 1) * LANES]
        qs_ref[h] = jnp.where((drow >= lo) & (drow < lo + HEAD_DIM), qp * (SCALE * LOG2E), 0.0).astype(BF16)
        kmm = jnp.where((lane >= lo) & (lane < lo + HEAD_DIM), km[:, p * LANES:(p + 1) * LANES], 0.0)
        gate = jnp.dot(kmm, qp, preferred_element_type=F32, precision=lax.Precision.HIGHEST)
        cnt = jnp.zeros((nb, blk), jnp.int32)
        for m in range(nb):
            gm = gate[m:m + 1, :]
            beats = (gm > gate) | ((gm == gate) & (bid > m))
            cnt = cnt + jnp.where(beats & (qb > m), 1, 0)
        sel_ref[h] = jnp.where((cnt < MOBA_TOPK) & (bid < qb), 1.0, 0.0)

    def step(own, kb):
        k = k_ref[0, pl.ds(pl.multiple_of(kb * blk, blk), blk), :]
        if own:
            key_i = lax.broadcasted_iota(jnp.int32, (blk, blk), 0)
            qry_i = lax.broadcasted_iota(jnp.int32, (blk, blk), 1)
            causal = key_i <= qry_i

        def logits(h):
            p = h // PAIR
            return _dot(k[:, p * LANES:(p + 1) * LANES], qs_ref[h])

        ahead = [logits(h) for h in range(LOGITS_AHEAD)]
        for h in range(N_HEADS):
            rows = slice(h * HEAD_DIM, (h + 1) * HEAD_DIM)
            s = ahead.pop(0)
            if h + LOGITS_AHEAD < N_HEADS:
                ahead.append(logits(h + LOGITS_AHEAD))
            vt = vt_ref[0, kb, rows, :]
            if own:
                s = jnp.where(causal, s, NEG)
                m_new = jnp.max(s, axis=0, keepdims=True)
                pr = jnp.exp2(s - m_new)
                l_ref[h] = jnp.sum(pr, axis=0, keepdims=True)
                acc_ref[rows, :] = _dot(vt, pr.astype(BF16))
            else:
                taken = sel_ref[h, pl.ds(kb, 1), :] > 0.0
                m_i = m_ref[h]
                m_new = jnp.where(taken, jnp.maximum(m_i, jnp.max(s, axis=0, keepdims=True)), m_i)
                alpha = jnp.exp2(m_i - m_new)
                pr = jnp.exp2(s - jnp.where(taken, m_new, -NEG))
                l_ref[h] = alpha * l_ref[h] + jnp.sum(pr, axis=0, keepdims=True)
                acc_ref[rows, :] = alpha * acc_ref[rows, :] + _dot(vt, pr.astype(BF16))
            m_ref[h] = m_new

    step(True, qb)

    def past(kb, carry):
        step(False, kb)
        return carry

    lax.fori_loop(0, qb, past, 0)
    outs = [acc_ref[h * HEAD_DIM:(h + 1) * HEAD_DIM, :] / l_ref[h] for h in range(N_HEADS)]
    o_ref[0] = jnp.concatenate(outs, axis=0).T


def _prompt_attn(qt, kbf, vt, km):
    b, nb, _, blk = qt.shape
    l = nb * blk
    return pl.pallas_call(
        _prompt_attn_kernel,
        grid=(b, nb),
        in_specs=[pl.BlockSpec((1, 1, WIDTH, blk), lambda i, j: (i, j, 0, 0)),
                  pl.BlockSpec((1, l, WIDTH), lambda i, j: (i, 0, 0)),
                  pl.BlockSpec((1, nb, WIDTH, blk), lambda i, j: (i, 0, 0, 0)),
                  pl.BlockSpec((1, nb, WIDTH), lambda i, j: (i, 0, 0))],
        out_specs=pl.BlockSpec((1, blk, WIDTH), lambda i, j: (i, j, 0)),
        out_shape=jax.ShapeDtypeStruct((b, l, WIDTH), F32),
        scratch_shapes=[pltpu.VMEM((N_HEADS, LANES, blk), BF16),
                        pltpu.VMEM((N_HEADS, nb, blk), F32),
                        pltpu.VMEM((N_HEADS, 1, blk), F32),
                        pltpu.VMEM((N_HEADS, 1, blk), F32),
                        pltpu.VMEM((WIDTH, blk), F32)],
        compiler_params=pltpu.CompilerParams(dimension_semantics=("parallel", "parallel"),
                                             vmem_limit_bytes=VMEM_LIMIT),
        name="prompt_attn",
    )(qt, kbf, vt, km)


def _out_mlp(x, a_n, b_out, sh_f, sc_f, g_m, g_f, onb, gffn, wout_ref, w1_ref, w2_ref):
    b_n = _rms_rows(b_out, onb).astype(BF16)
    mixed = _dot(a_n, wout_ref[0:WIDTH, :]) + _dot(b_n, wout_ref[WIDTH:2 * WIDTH, :])
    x1 = x + g_m * mixed
    h2 = (_rms_rows(x1, gffn) * (1.0 + sc_f) + sh_f).astype(BF16)
    acc = jnp.zeros_like(x1)
    for c in range(D_FF // D_MODEL):
        t = jnp.maximum(_dot(h2, w1_ref[:, c * D_MODEL:(c + 1) * D_MODEL]), 0.0)
        acc = acc + _dot((t * t).astype(BF16), w2_ref[c * D_MODEL:(c + 1) * D_MODEL, :])
    return x1 + g_f * acc


DIM_GROUPS = HEAD_DIM // SUBLANES


def _block_gate_partials(q_row, page_refs):
    qt = _row_as_columns(q_row)
    n_blocks = len(page_refs) // PAGES_PER_BLOCK
    sub = lax.broadcasted_iota(jnp.int32, (N_HEADS, LANES), 0)
    blocks = [jnp.zeros((N_HEADS, LANES), F32)] * n_blocks
    for h in range(N_HEADS):
        tot = [None] * n_blocks
        for g in range(DIM_GROUPS):
            rows = slice(g * SUBLANES, (g + 1) * SUBLANES)
            qv = qt[h * HEAD_DIM + g * SUBLANES:h * HEAD_DIM + (g + 1) * SUBLANES, :]
            for t in range(n_blocks):
                pages = sum(page_refs[PAGES_PER_BLOCK * t + u][0, 0, h, rows, :] for u in range(1, PAGES_PER_BLOCK))
                term = (page_refs[PAGES_PER_BLOCK * t][0, 0, h, rows, :] + pages) * qv
                tot[t] = term if tot[t] is None else tot[t] + term
        for t in range(n_blocks):
            blocks[t] = jnp.where(sub == h, jnp.sum(tot[t], axis=0, keepdims=True), blocks[t])
    return jnp.stack(blocks, axis=0)


def _prompt_out_kernel(steps_per_seq, pt_ref, x_ref, an_ref, b_ref, mod_ref, onb_ref, gffn_ref,
                       wout_ref, w1_ref, w2_ref, qs_ref, *refs):
    page_refs, (y_ref, g_ref) = refs[:-2], refs[-2:]
    step = pl.program_id(0) * pl.num_programs(1) + pl.program_id(1)
    y_ref[0] = _out_mlp(x_ref[0], an_ref[0], b_ref[0],
                        mod_ref[0, 3:4, :], mod_ref[0, 4:5, :], mod_ref[0, 2:3, :], mod_ref[0, 5:6, :],
                        onb_ref[...], gffn_ref[...], wout_ref, w1_ref, w2_ref)
    g_ref[0] = _block_gate_partials(qs_ref[pl.ds(step // steps_per_seq, 1), :], page_refs)


def _prompt_out(l_idx, x, a_n, b_out, mod, on_b, g_ffn, w_out, w1, w2, q_s, page_table, cache_kt, tm=256):
    b, l, _ = x.shape
    nt = l // tm
    ns, n_pages = page_table.shape
    steps_per_seq = (b * nt) // ns
    pages_per_step = n_pages // steps_per_seq
    blocks_per_step = pages_per_step // PAGES_PER_BLOCK
    assert steps_per_seq * ns == b * nt and pages_per_step * steps_per_seq == n_pages
    assert blocks_per_step * PAGES_PER_BLOCK == pages_per_step

    def page_map(t):
        return lambda i, j, pt: (l_idx, pt[(i * nt + j) * pages_per_step + t], 0, 0, 0)

    const2 = lambda i, j, pt: (0, 0)
    rows = lambda i, j, pt: (i, j, 0)
    once = pl.Buffered(1)
    gate_block = (blocks_per_step, N_HEADS, LANES)
    grid_spec = pltpu.PrefetchScalarGridSpec(
        num_scalar_prefetch=1,
        grid=(b, nt),
        in_specs=[pl.BlockSpec((1, tm, D_MODEL), rows),
                  pl.BlockSpec((1, tm, WIDTH), rows),
                  pl.BlockSpec((1, tm, WIDTH), rows),
                  pl.BlockSpec((1, 6, D_MODEL), lambda i, j, pt: (i, 0, 0)),
                  pl.BlockSpec((1, WIDTH), const2),
                  pl.BlockSpec((1, D_MODEL), const2),
                  pl.BlockSpec((2 * WIDTH, D_MODEL), const2, pipeline_mode=once),
                  pl.BlockSpec((D_MODEL, D_FF), const2, pipeline_mode=once),
                  pl.BlockSpec((D_FF, D_MODEL), const2, pipeline_mode=once),
                  pl.BlockSpec((ns, WIDTH), const2)]
                 + [pl.BlockSpec((1, 1, N_HEADS, HEAD_DIM, PAGE_SIZE), page_map(t)) for t in range(pages_per_step)],
        out_specs=[pl.BlockSpec((1, tm, D_MODEL), rows),
                   pl.BlockSpec((1,) + gate_block, lambda i, j, pt: (i * nt + j, 0, 0, 0))])
    y, g = pl.pallas_call(
        functools.partial(_prompt_out_kernel, steps_per_seq),
        grid_spec=grid_spec,
        out_shape=[jax.ShapeDtypeStruct(x.shape, F32),
                   jax.ShapeDtypeStruct((b * nt,) + gate_block, F32)],
        compiler_params=pltpu.CompilerParams(dimension_semantics=("parallel", "parallel"),
                                             vmem_limit_bytes=VMEM_LIMIT),
        name="prompt_out",
    )(page_table.reshape(-1), x, a_n, b_out, mod, on_b, g_ffn, w_out, w1, w2, q_s,
      *([cache_kt] * pages_per_step))
    return y, g.reshape(ns, n_pages // PAGES_PER_BLOCK, N_HEADS, LANES)


def _sample_out_kernel(x_ref, an_ref, b_ref, mod_ref, onb_ref, gffn_ref, wout_ref, w1_ref, w2_ref, y_ref):
    y_ref[...] = _out_mlp(x_ref[...], an_ref[...], b_ref[...],
                          mod_ref[3], mod_ref[4], mod_ref[2], mod_ref[5],
                          onb_ref[...], gffn_ref[...], wout_ref, w1_ref, w2_ref)


def _sample_out(x, a_n, b_out, mod_t, on_b, g_ffn, w_out, w1, w2):
    return pl.pallas_call(
        _sample_out_kernel,
        out_shape=jax.ShapeDtypeStruct(x.shape, F32),
        compiler_params=pltpu.CompilerParams(vmem_limit_bytes=VMEM_LIMIT),
        name="sample_out",
    )(x, a_n, b_out, mod_t, on_b, g_ffn, w_out, w1, w2)


def _sample_in_kernel(x_ref, mod_ref, gmix_ref, win_ref, bd_ref, qg_ref, kg_ref, vg_ref,
                      ws0_ref, bs0_ref, ona_ref,
                      q_ref, k_ref, v_ref, cv_ref, an_ref):
    z = _in_proj(x_ref[...], mod_ref[0], mod_ref[1], gmix_ref[...], win_ref[...])
    bd = bd_ref[...]
    ua, va = z[:, 0:WIDTH], z[:, WIDTH:2 * WIDTH]
    q, k, v = z[:, 2 * WIDTH:3 * WIDTH], z[:, 3 * WIDTH:4 * WIDTH], z[:, 4 * WIDTH:5 * WIDTH]
    q_ref[...] = _head_rms(q, bd, qg_ref[...])
    k_ref[...] = _head_rms(k, bd, kg_ref[...])
    v_ref[...] = v
    vn = _head_rms(jax.nn.gelu(va), bd, vg_ref[...])
    cv_ref[...] = vn
    mixed = ws0_ref[...] * vn + bs0_ref[...]
    an_ref[...] = _rms_rows(jax.nn.gelu(ua) * mixed, ona_ref[...]).astype(BF16)


def _sample_in(x, mod_t, g_mix, w_in, bd, qg, kg, vg, ws0, bs0, on_a):
    n = x.shape[0]
    row = jax.ShapeDtypeStruct((n, WIDTH), F32)
    return pl.pallas_call(
        _sample_in_kernel,
        out_shape=[row, row, row, row, jax.ShapeDtypeStruct((n, WIDTH), BF16)],
        compiler_params=pltpu.CompilerParams(vmem_limit_bytes=VMEM_LIMIT),
        name="sample_in",
    )(x, mod_t, g_mix, w_in, bd, qg, kg, vg, ws0, bs0, on_a)


def _row_as_columns(row):
    return jnp.broadcast_to(row, (LANES, WIDTH)).T


def _sample_top3_kernel(g_ref, sel_ref):
    nb = g_ref.shape[1]
    gate = jnp.sum(g_ref[0], axis=2, keepdims=True) * (1.0 / MOBA_BLOCK)
    bidf = lax.broadcasted_iota(jnp.int32, gate.shape, 0).astype(F32)
    picks = []
    for _ in range(MOBA_TOPK):
        mx = jnp.max(gate, axis=0, keepdims=True)
        idx = jnp.min(jnp.where(gate == mx, bidf, float(nb)), axis=0, keepdims=True)
        picks.append(idx)
        gate = jnp.where(bidf == idx, -jnp.inf, gate)
    sel = jnp.concatenate(picks, axis=0)
    sel_ref[0] = jnp.broadcast_to(sel, sel_ref.shape[1:]).astype(jnp.int32)


def _sample_top3(g):
    n = g.shape[0]
    sel_block = (MOBA_TOPK, N_HEADS, LANES)
    sel = pl.pallas_call(
        _sample_top3_kernel,
        grid=(n,),
        in_specs=[pl.BlockSpec((1,) + g.shape[1:], lambda s: (s, 0, 0, 0))],
        out_specs=pl.BlockSpec((1,) + sel_block, lambda s: (s, 0, 0, 0)),
        out_shape=jax.ShapeDtypeStruct((n,) + sel_block, jnp.int32),
        compiler_params=pltpu.CompilerParams(dimension_semantics=("parallel",),
                                             vmem_limit_bytes=VMEM_LIMIT),
        name="sample_top3",
    )(g)
    return sel[:, :, :, 0]


SLABS = MOBA_TOPK * PAGES_PER_BLOCK
N_SLABS = N_HEADS * SLABS


def _sample_attn_kernel(l, n_pages, sel_ref, pt_ref, q_ref, kn_ref, vn_ref, ck_ref, cv_ref, o_ref,
                        kbuf, vbuf, sem):
    s = pl.program_id(0)
    n = pl.num_programs(0)
    slot = lax.rem(s, 2)

    def slab_copies(page, h, i, slot_):
        return (pltpu.make_async_copy(ck_ref.at[l, page, h], kbuf.at[slot_, i], sem.at[0, slot_]),
                pltpu.make_async_copy(cv_ref.at[l, page, h], vbuf.at[slot_, i], sem.at[1, slot_]))

    def start_fetch(seq, slot_):
        for h in range(N_HEADS):
            for r in range(MOBA_TOPK):
                blk = sel_ref[(seq * MOBA_TOPK + r) * N_HEADS + h]
                for u in range(PAGES_PER_BLOCK):
                    page = pt_ref[seq * n_pages + blk * PAGES_PER_BLOCK + u]
                    for w, c in enumerate(slab_copies(page, h, h * SLABS + r * PAGES_PER_BLOCK + u, slot_)):
                        c.start(priority=(u + w) % 2)

    def wait_fetch(slot_):
        for i in range(N_SLABS):
            for c in slab_copies(0, 0, i, slot_):
                c.wait()

    @pl.when(s == 0)
    def _():
        start_fetch(0, 0)

    @pl.when(s + 1 < n)
    def _():
        start_fetch(s + 1, 1 - slot)

    wait_fetch(slot)

    qt = _row_as_columns(q_ref[pl.ds(s, 1), :] * SCALE)
    knt = _row_as_columns(kn_ref[pl.ds(s, 1), :])
    lane_head = lax.broadcasted_iota(jnp.int32, (1, WIDTH), 1) // HEAD_DIM
    pself_row = jnp.zeros((1, WIDTH), F32)
    l_row = jnp.zeros((1, WIDTH), F32)
    accs = []
    for h in range(N_HEADS):
        rows = slice(h * HEAD_DIM, (h + 1) * HEAD_DIM)
        qh = qt[rows]
        s_self = jnp.sum(qh * knt[rows], axis=0, keepdims=True)
        logits = [jnp.sum(kbuf[slot, h * SLABS + t] * qh, axis=0, keepdims=True) for t in range(SLABS)]
        mx = logits[0]
        for s_t in logits[1:]:
            mx = jnp.maximum(mx, s_t)
        m = jnp.maximum(s_self, jnp.max(mx, axis=1, keepdims=True))
        p_self = jnp.exp(s_self - m)
        l_h = p_self
        acc = jnp.zeros((HEAD_DIM, LANES), F32)
        for t, s_t in enumerate(logits):
            p = jnp.exp(s_t - m)
            l_h = l_h + jnp.sum(p, axis=1, keepdims=True)
            acc = acc + vbuf[slot, h * SLABS + t] * p
        accs.append(acc)
        pself_row = jnp.where(lane_head == h, p_self[:, 0:1], pself_row)
        l_row = jnp.where(lane_head == h, l_h[:, 0:1], l_row)
    pv = jnp.sum(jnp.concatenate(accs, axis=0).T, axis=0, keepdims=True)
    o_ref[pl.ds(s, 1), :] = (pv + pself_row * vn_ref[pl.ds(s, 1), :]) / l_row


def _sample_attn(l, sel, page_table, q, k_new, v_new, cache_kt, cache_vt):
    n = q.shape[0]
    full = pl.BlockSpec((n, WIDTH), lambda s, sel_, pt: (0, 0))
    grid_spec = pltpu.PrefetchScalarGridSpec(
        num_scalar_prefetch=2,
        grid=(n,),
        in_specs=[full, full, full, pl.BlockSpec(memory_space=pl.ANY), pl.BlockSpec(memory_space=pl.ANY)],
        out_specs=full,
        scratch_shapes=[pltpu.VMEM((2, N_SLABS, HEAD_DIM, PAGE_SIZE), F32),
                        pltpu.VMEM((2, N_SLABS, HEAD_DIM, PAGE_SIZE), F32),
                        pltpu.SemaphoreType.DMA((2, 2))])
    return pl.pallas_call(
        functools.partial(_sample_attn_kernel, l, page_table.shape[1]),
        grid_spec=grid_spec,
        out_shape=jax.ShapeDtypeStruct((n, WIDTH), F32),
        compiler_params=pltpu.CompilerParams(dimension_semantics=("arbitrary",),
                                             vmem_limit_bytes=VMEM_LIMIT),
        name="sample_attn",
    )(sel.reshape(-1), page_table.reshape(-1), q, k_new, v_new, cache_kt, cache_vt)


def _head_tile(g):
    return jnp.tile(g, N_HEADS).reshape(1, WIDTH)


def _layer(l, x_prompt, x_sample, cache_kt, cache_vt, page_table, c_prompt, c_sample,
           norm_mix_g, norm_ffn_g, w_ada, b_ada, w_in, qn_g, kn_g, vn_g, w_s, b_s,
           on_a, on_b, w_out, w1, w2):
    nb_, seq = x_prompt.shape[0], x_prompt.shape[1]
    ns = x_sample.shape[0]

    w_in_b, w_out_b = w_in[l].astype(BF16), w_out[l].astype(BF16)
    w1_b, w2_b = w1[l].astype(BF16), w2[l].astype(BF16)
    g_mix, g_ffn = norm_mix_g[l].reshape(1, D_MODEL), norm_ffn_g[l].reshape(1, D_MODEL)
    qg, kg, vg = _head_tile(qn_g[l]), _head_tile(kn_g[l]), _head_tile(vn_g[l])
    ona, onb = on_a[l].reshape(1, WIDTH), on_b[l].reshape(1, WIDTH)
    head_of = jnp.arange(2 * LANES) // HEAD_DIM
    bd = (head_of[:, None] == head_of[None, :]).astype(BF16)
    bias = jnp.repeat(b_s[l].T, HEAD_DIM, axis=1)
    ws0 = jnp.repeat(w_s[l][:, 0, 0], HEAD_DIM).reshape(1, WIDTH)
    bs0 = jnp.repeat(b_s[l][:, 0], HEAD_DIM).reshape(1, WIDTH)

    mod = _ada(jnp.concatenate([c_prompt, c_sample], axis=0), w_ada[l], b_ada[l])
    mod_p = mod[:nb_].reshape(nb_, 6, D_MODEL)
    mod_s = mod[nb_:].reshape(ns, 6, D_MODEL).transpose(1, 0, 2)

    xs = x_sample.reshape(ns, D_MODEL)
    q_s, k_s, v_s, cv_s, an_s = _sample_in(xs, mod_s, g_mix, w_in_b, bd, qg, kg, vg, ws0, bs0, ona)

    kt_p, vt_p, kbf, km, qt, vt, an_p = _prompt_in(x_prompt, mod_p, g_mix, w_in_b, bd, qg, kg, vg,
                                                   w_s[l], bias, ona)
    b_p = _prompt_attn(qt, kbf, vt, km)
    y_p, gate_parts = _prompt_out(l, x_prompt, an_p, b_p, mod_p, onb, g_ffn, w_out_b, w1_b, w2_b,
                                  q_s, page_table, cache_kt)

    sel = _sample_top3(gate_parts)
    b_s_out = _sample_attn(l, sel, page_table, q_s, k_s, v_s, cache_kt, cache_vt)
    y_s = _sample_out(xs, an_s, b_s_out, mod_s, onb, g_ffn, w_out_b, w1_b, w2_b)

    hshape = (N_HEADS, HEAD_DIM)

    def rows_from_cols(t):
        return t.reshape(nb_, *hshape, seq).transpose(0, 3, 1, 2)

    return (y_p, y_s.reshape(x_sample.shape), rows_from_cols(kt_p), rows_from_cols(vt_p),
            k_s.reshape(ns, 1, *hshape), v_s.reshape(ns, 1, *hshape), cv_s.reshape(ns, 1, *hshape))


def kernel(x_prompt, x_sample, cache_k, cache_v, page_table, c_prompt, c_sample, norm_mix_g, norm_ffn_g,
           w_ada, b_ada, w_in, qn_g, kn_g, vn_g, w_s, b_s, on_a, on_b, w_out, w1, w2):
    depth = w_in.shape[0]
    cache_kt = jnp.transpose(cache_k, (0, 1, 3, 4, 2))
    cache_vt = jnp.transpose(cache_v, (0, 1, 3, 4, 2))
    x_p, x_s = x_prompt, x_sample
    kp, vp, ks, vs, cv = [], [], [], [], []
    for l in range(depth):
        x_p, x_s, k_p, v_p, k_s, v_s, cv_s = _layer(
            l, x_p, x_s, cache_kt, cache_vt, page_table, c_prompt, c_sample, norm_mix_g, norm_ffn_g,
            w_ada, b_ada, w_in, qn_g, kn_g, vn_g, w_s, b_s, on_a, on_b, w_out, w1, w2)
        kp.append(k_p); vp.append(v_p); ks.append(k_s); vs.append(v_s); cv.append(cv_s)
    return (x_p, x_s, jnp.stack(kp), jnp.stack(vp), jnp.stack(ks), jnp.stack(vs), jnp.stack(cv))
```
